```python
import math
import jax, jax.numpy as jnp
from jax import lax
import numpy as np

D_MODEL = 1024
BATCH = 8
SEQ = 2048
DEPTH = 4

N_MIXERS = 3
N_RET = (DEPTH + 2) // 3
N_GMLP = (DEPTH + 1) // 3
N_CONV = DEPTH // 3

RET_HEADS = 4
RET_DK = D_MODEL // RET_HEADS
RET_DV = 2 * D_MODEL // RET_HEADS
RET_IN = 2 * D_MODEL + 2 * (2 * D_MODEL)
RET_CHUNK = 128
ROPE_BASE = 10000.0

GMLP_D_FFN = 6 * D_MODEL
GMLP_HALF = GMLP_D_FFN // 2
GMLP_GROUPS = 4
GMLP_GROUP_DIM = GMLP_HALF // GMLP_GROUPS
GMLP_CHUNK = 128

CONV_WIDTH = 3

MEM_LEN = 256
XA_HEADS = 4
XA_DH = D_MODEL // XA_HEADS

D_FF = 4 * D_MODEL

NORM_EPS = 1e-6
GN_EPS = 1e-5

kernel_name = "hybrid_retention_gmlp_shortconv_trunk"


def rms_norm(x, g):
    xf = x.astype(jnp.float32)
    y = xf * lax.rsqrt(jnp.mean(xf * xf, axis=-1, keepdims=True) + NORM_EPS)
    return (y * g.astype(jnp.float32)).astype(x.dtype)


def layer_norm(x, g, b):
    xf = x.astype(jnp.float32)
    mu = jnp.mean(xf, axis=-1, keepdims=True)
    var = jnp.mean(jnp.square(xf - mu), axis=-1, keepdims=True)
    y = (xf - mu) * lax.rsqrt(var + GN_EPS)
    return (y * g.astype(jnp.float32) + b.astype(jnp.float32)).astype(x.dtype)


def rotary(t, positions):
    half = t.shape[-1] // 2
    inv_freq = ROPE_BASE ** (-jnp.arange(half, dtype=jnp.float32) / half)
    ang = positions.astype(jnp.float32)[..., None] * inv_freq
    cos = jnp.cos(ang)[:, :, None, :]
    sin = jnp.sin(ang)[:, :, None, :]
    tf = t.astype(jnp.float32)
    t1, t2 = tf[..., :half], tf[..., half:]
    out = jnp.concatenate([t1 * cos - t2 * sin, t2 * cos + t1 * sin], axis=-1)
    return out.astype(t.dtype)


def retention_mixer(xn, positions, w_in, gn_g, w_out):
    B, S, _ = xn.shape
    proj = xn @ w_in
    q, k, v, g = jnp.split(proj, [D_MODEL, 2 * D_MODEL, 4 * D_MODEL], axis=-1)
    q = rotary(q.reshape(B, S, RET_HEADS, RET_DK), positions)
    k = rotary(k.reshape(B, S, RET_HEADS, RET_DK), positions) * (RET_DK ** -0.5)
    v = v.reshape(B, S, RET_HEADS, RET_DV)
    nc = S // RET_CHUNK

    def to_chunks(t):
        return t.reshape(B, nc, RET_CHUNK, RET_HEADS, t.shape[-1]).transpose(1, 0, 3, 2, 4)

    qc, kc, vc = to_chunks(q), to_chunks(k), to_chunks(v)

    log_g = jnp.log(1.0 - 2.0 ** (-5.0 - jnp.arange(RET_HEADS, dtype=jnp.float32)))
    idx = jnp.arange(RET_CHUNK, dtype=jnp.float32)
    rel = idx[:, None] - idx[None, :]
    intra = jnp.where(rel[None] >= 0, jnp.exp(rel[None] * log_g[:, None, None]), 0.0)
    xi = jnp.exp((idx[None] + 1.0) * log_g[:, None])
    zeta = jnp.exp((RET_CHUNK - 1.0 - idx[None]) * log_g[:, None])
    chunk_decay = jnp.exp(RET_CHUNK * log_g)

    def step(R, qkv):
        qi, ki, vi = qkv
        s = jnp.einsum('bhtd,bhsd->bhts', qi, ki) * intra[None]
        o_intra = jnp.einsum('bhts,bhse->bhte', s, vi)
        o_inter = jnp.einsum('bhtd,bhde->bhte', qi, R) * xi[None, :, :, None]
        R_new = R * chunk_decay[None, :, None, None] + jnp.einsum(
            'bhsd,bhse->bhde', ki * zeta[None, :, :, None], vi)
        return R_new, o_intra + o_inter

    R0 = jnp.zeros((B, RET_HEADS, RET_DK, RET_DV), jnp.float32)
    _, o = lax.scan(step, R0, (qc, kc, vc))
    o = o.transpose(1, 0, 3, 2, 4).reshape(B, S, RET_HEADS, RET_DV)
    mu = jnp.mean(o, axis=-1, keepdims=True)
    var = jnp.mean(jnp.square(o - mu), axis=-1, keepdims=True)
    o = (o - mu) * lax.rsqrt(var + GN_EPS)
    o = (o.reshape(B, S, 2 * D_MODEL) * gn_g.astype(jnp.float32)).astype(xn.dtype)
    return (jax.nn.silu(g) * o) @ w_out


def gmlp_mixer(xn, w_in, ln_g, ln_b, w_s, b_s, w_out):
    B, S, _ = xn.shape
    z = jax.nn.gelu(xn @ w_in, approximate=False)
    u, v = jnp.split(z, 2, axis=-1)
    v = layer_norm(v, ln_g, ln_b)
    nc = S // GMLP_CHUNK
    v = v.reshape(B, nc, GMLP_CHUNK, GMLP_GROUPS, GMLP_GROUP_DIM)
    mask = jnp.tril(jnp.ones((GMLP_CHUNK, GMLP_CHUNK), dtype=w_s.dtype))
    sv = jnp.einsum('gts,bnsgc->bntgc', w_s * mask[None], v) + b_s.T[None, None, :, :, None]
    return (u * sv.reshape(B, S, GMLP_HALF)) @ w_out


def short_conv_mixer(xn, w_in, conv_k, w_out):
    b_gate, c_gate, h = jnp.split(xn @ w_in, 3, axis=-1)
    z = lax.conv_general_dilated(
        c_gate * h, conv_k[:, None, :].astype(h.dtype),
        window_strides=(1,), padding=[(CONV_WIDTH - 1, 0)],
        dimension_numbers=('NWC', 'WIO', 'NWC'), feature_group_count=D_MODEL)
    return (b_gate * z) @ w_out


def memory_cross_attention(xn, mem_n, w_q, w_kv, w_o):
    B, S, _ = xn.shape
    q = (xn @ w_q).reshape(B, S, XA_HEADS, XA_DH)
    k, v = jnp.split(mem_n @ w_kv, 2, axis=-1)
    k = k.reshape(B, MEM_LEN, XA_HEADS, XA_DH)
    v = v.reshape(B, MEM_LEN, XA_HEADS, XA_DH)
    s = jnp.einsum('bshd,bmhd->bhsm', q, k).astype(jnp.float32) * (XA_DH ** -0.5)
    p = jax.nn.softmax(s, axis=-1).astype(v.dtype)
    o = jnp.einsum('bhsm,bmhd->bshd', p, v).reshape(B, S, XA_HEADS * XA_DH)
    return o @ w_o


def sq_relu_mlp(xn, w1, w2):
    return jnp.square(jax.nn.relu(xn @ w1)) @ w2


def setup_inputs(seed: int = 0) -> dict:
    key = jax.random.key(seed)
    ks = iter(jax.random.split(key, 32))

    def w(shape, fan_in):
        return jax.random.normal(next(ks), shape, jnp.float32) * (fan_in ** -0.5)

    def gain(shape):
        return 1.0 + 0.02 * jax.random.normal(next(ks), shape, jnp.float32)

    x = jax.random.normal(next(ks), (BATCH, SEQ, D_MODEL), jnp.float32)
    mem = jax.random.normal(next(ks), (BATCH, MEM_LEN, D_MODEL), jnp.float32)
    offs = jax.random.randint(next(ks), (BATCH, 1), 0, 4096, dtype=jnp.int32)
    positions = jnp.arange(SEQ, dtype=jnp.int32)[None, :] + offs
    return {
        "x": x,
        "mem": mem,
        "positions": positions,
        "norm_mix_g": gain((DEPTH, D_MODEL)),
        "norm_xa_g": gain((DEPTH, D_MODEL)),
        "norm_mem_g": gain((DEPTH, D_MODEL)),
        "xa_w_q": w((DEPTH, D_MODEL, XA_HEADS * XA_DH), D_MODEL),
        "xa_w_kv": w((DEPTH, D_MODEL, 2 * XA_HEADS * XA_DH), D_MODEL),
        "xa_w_o": w((DEPTH, XA_HEADS * XA_DH, D_MODEL), XA_HEADS * XA_DH),
        "norm_ffn_g": gain((DEPTH, D_MODEL)),
        "ffn_w1": w((DEPTH, D_MODEL, D_FF), D_MODEL),
        "ffn_w2": w((DEPTH, D_FF, D_MODEL), D_FF),
        "ret_w_in": w((N_RET, D_MODEL, RET_IN), D_MODEL),
        "ret_gn_g": gain((N_RET, 2 * D_MODEL)),
        "ret_w_out": w((N_RET, 2 * D_MODEL, D_MODEL), 2 * D_MODEL),
        "gmlp_w_in": w((N_GMLP, D_MODEL, GMLP_D_FFN), D_MODEL),
        "gmlp_ln_g": gain((N_GMLP, GMLP_HALF)),
        "gmlp_ln_b": 0.02 * jax.random.normal(next(ks), (N_GMLP, GMLP_HALF), jnp.float32),
        "gmlp_w_s": w((N_GMLP, GMLP_GROUPS, GMLP_CHUNK, GMLP_CHUNK), GMLP_CHUNK),
        "gmlp_b_s": gain((N_GMLP, GMLP_GROUPS, GMLP_CHUNK)),
        "gmlp_w_out": w((N_GMLP, GMLP_HALF, D_MODEL), GMLP_HALF),
        "conv_w_in": w((N_CONV, D_MODEL, 3 * D_MODEL), D_MODEL),
        "conv_k": w((N_CONV, CONV_WIDTH, D_MODEL), CONV_WIDTH),
        "conv_w_out": w((N_CONV, D_MODEL, D_MODEL), D_MODEL),
        "norm_f_g": gain((D_MODEL,)),
    }


def reference(x, mem, positions, norm_mix_g, norm_xa_g, norm_mem_g, xa_w_q, xa_w_kv, xa_w_o,
              norm_ffn_g, ffn_w1, ffn_w2, ret_w_in, ret_gn_g, ret_w_out,
              gmlp_w_in, gmlp_ln_g, gmlp_ln_b, gmlp_w_s, gmlp_b_s, gmlp_w_out,
              conv_w_in, conv_k, conv_w_out, norm_f_g):
    h = x
    for i in range(DEPTH):
        kind, j = i % N_MIXERS, i // N_MIXERS
        hn = rms_norm(h, norm_mix_g[i])
        if kind == 0:
            mix = retention_mixer(hn, positions, ret_w_in[j], ret_gn_g[j], ret_w_out[j])
        elif kind == 1:
            mix = gmlp_mixer(hn, gmlp_w_in[j], gmlp_ln_g[j], gmlp_ln_b[j],
                             gmlp_w_s[j], gmlp_b_s[j], gmlp_w_out[j])
        else:
            mix = short_conv_mixer(hn, conv_w_in[j], conv_k[j], conv_w_out[j])
        h = h + mix
        h = h + memory_cross_attention(rms_norm(h, norm_xa_g[i]), rms_norm(mem, norm_mem_g[i]),
                                       xa_w_q[i], xa_w_kv[i], xa_w_o[i])
        h = h + sq_relu_mlp(rms_norm(h, norm_ffn_g[i]), ffn_w1[i], ffn_w2[i])
    return rms_norm(h, norm_f_g)
```

```python
import functools

import jax
import jax.numpy as jnp
from jax import lax
from jax.experimental import pallas as pl
from jax.experimental.pallas import tpu as pltpu

D_MODEL = 1024
BATCH = 8
SEQ = 2048
DEPTH = 4
N_MIXERS = 3

RET_HEADS = 4
RET_DK = D_MODEL // RET_HEADS
RET_DV = 2 * D_MODEL // RET_HEADS
ROPE_BASE = 10000.0
ROPE_HALF = RET_DK // 2

GMLP_HALF = 3 * D_MODEL
GMLP_GROUPS = 4
GMLP_GROUP_DIM = GMLP_HALF // GMLP_GROUPS
GMLP_CHUNK = 128

CONV_WIDTH = 3

MEM_LEN = 256
XA_HEADS = 4
XA_DH = D_MODEL // XA_HEADS

D_FF = 4 * D_MODEL

NORM_EPS = 1e-6
GN_EPS = 1e-5

RET_TILE = 256
GMLP_TILE = 256
CONV_TILE = 512
XA_TILE = 512
FFN_TILE = 512
TRIG_TILE = 512
FFN_COLS = 1024
SUBLANES = 8

VMEM_LIMIT_BYTES = 56 * 1024 * 1024

BF16 = jnp.bfloat16
F32 = jnp.float32


def _dot(a, b):
    return jnp.dot(a, b, preferred_element_type=F32)


def _dot_nt(a, b):
    return lax.dot_general(a, b, (((1,), (1,)), ((), ())), preferred_element_type=F32)


def _dot_tn(a, b):
    return lax.dot_general(a, b, (((0,), (0,)), ((), ())), preferred_element_type=F32)


def _rms(x, g):
    return x * lax.rsqrt(jnp.mean(x * x, axis=-1, keepdims=True) + NORM_EPS) * g


def _const_spec(shape):
    zeros = (0,) * len(shape)
    return pl.BlockSpec(shape, lambda *_: zeros, pipeline_mode=pl.Buffered(1))


def _layer_spec(shape, layer):
    idx = (layer,) + (0,) * len(shape)
    return pl.BlockSpec((None,) + shape, lambda *_: idx, pipeline_mode=pl.Buffered(1))


def _params(*semantics):
    return pltpu.CompilerParams(dimension_semantics=semantics,
                                vmem_limit_bytes=VMEM_LIMIT_BYTES)


def _trig_kernel(pos_ref, inv_ref, cos_ref, sin_ref):
    ang = pos_ref[...].astype(F32) * inv_ref[...]
    cos_ref[...] = jnp.cos(ang)
    sin_ref[...] = jnp.sin(ang)


def _rotary_tables(positions):
    inv_freq = ROPE_BASE ** (-jnp.arange(ROPE_HALF, dtype=F32) / ROPE_HALF)
    out = jax.ShapeDtypeStruct((BATCH, SEQ, ROPE_HALF), F32)
    tile = pl.BlockSpec((None, TRIG_TILE, ROPE_HALF), lambda b, j: (b, j, 0))
    return pl.pallas_call(
        _trig_kernel,
        grid=(BATCH, SEQ // TRIG_TILE),
        in_specs=[pl.BlockSpec((None, TRIG_TILE, 1), lambda b, j: (b, j, 0)),
                  _const_spec((1, ROPE_HALF))],
        out_specs=[tile, tile],
        out_shape=[out, out],
        compiler_params=_params("parallel", "parallel"),
        name="rotary_tables",
    )(positions.reshape(BATCH, SEQ, 1), inv_freq.reshape(1, ROPE_HALF))


def _retention_kernel(decay_ref, h_ref, cos_ref, sin_ref, g_ref, w_in_ref, intra_ref, xi_ref,
                      zeta_ref, gn_ref, w_out_ref, o_ref, state_ref, gated_ref):
    @pl.when(pl.program_id(1) == 0)
    def _():
        state_ref[...] = jnp.zeros_like(state_ref)

    x = h_ref[...]
    xn = _rms(x, g_ref[...]).astype(BF16)
    cos = cos_ref[...]
    sin = sin_ref[...]

    def rotate(t):
        t1, t2 = t[:, :ROPE_HALF], t[:, ROPE_HALF:]
        return jnp.concatenate([t1 * cos - t2 * sin, t2 * cos + t1 * sin], axis=-1)

    k_off, v_off, gate_off = D_MODEL, 2 * D_MODEL, 4 * D_MODEL
    for hd in range(RET_HEADS):
        dk = slice(hd * RET_DK, (hd + 1) * RET_DK)
        dv = slice(hd * RET_DV, (hd + 1) * RET_DV)
        q = rotate(_dot(xn, w_in_ref[:, dk]))
        k = rotate(_dot(xn, w_in_ref[:, k_off + hd * RET_DK:k_off + (hd + 1) * RET_DK]))
        k = k * (RET_DK ** -0.5)
        v = _dot(xn, w_in_ref[:, v_off + hd * RET_DV:v_off + (hd + 1) * RET_DV]).astype(BF16)
        qb = q.astype(BF16)
        state = state_ref[hd]
        s = _dot_nt(qb, k.astype(BF16)) * intra_ref[hd]
        o = _dot(s.astype(BF16), v) + _dot(qb, state.astype(BF16)) * xi_ref[hd]
        state_ref[hd] = state * decay_ref[hd] + _dot_tn((k * zeta_ref[hd]).astype(BF16), v)
        mu = jnp.mean(o, axis=-1, keepdims=True)
        var = jnp.mean(jnp.square(o - mu), axis=-1, keepdims=True)
        on = (o - mu) * lax.rsqrt(var + GN_EPS) * gn_ref[:, dv]
        gate = _dot(xn, w_in_ref[:, gate_off + hd * RET_DV:gate_off + (hd + 1) * RET_DV])
        gated_ref[:, dv] = (jax.nn.silu(gate) * on).astype(BF16)
    o_ref[...] = x + _dot(gated_ref[...], w_out_ref[...])


def _retention_layer(h, cos, sin, g_mix, w_in, gn_g, w_out):
    c = RET_TILE
    log_g = jnp.log(1.0 - 2.0 ** (-5.0 - jnp.arange(RET_HEADS, dtype=F32)))
    idx = jnp.arange(c, dtype=F32)
    rel = idx[:, None] - idx[None, :]
    intra = jnp.where(rel[None] >= 0, jnp.exp(rel[None] * log_g[:, None, None]), 0.0)
    xi = jnp.exp((idx[None] + 1.0) * log_g[:, None])[..., None]
    zeta = jnp.exp((c - 1.0 - idx[None]) * log_g[:, None])[..., None]
    chunk_decay = jnp.exp(c * log_g)

    tile = pl.BlockSpec((None, c, D_MODEL), lambda b, j: (b, j, 0))
    trig = pl.BlockSpec((None, c, ROPE_HALF), lambda b, j: (b, j, 0))
    return pl.pallas_call(
        _retention_kernel,
        grid=(BATCH, SEQ // c),
        in_specs=[pl.BlockSpec(memory_space=pltpu.SMEM),
                  tile, trig, trig,
                  _const_spec((1, D_MODEL)),
                  _const_spec(w_in.shape),
                  _const_spec((RET_HEADS, c, c)),
                  _const_spec((RET_HEADS, c, 1)),
                  _const_spec((RET_HEADS, c, 1)),
                  _const_spec((1, 2 * D_MODEL)),
                  _const_spec(w_out.shape)],
        out_specs=tile,
        out_shape=jax.ShapeDtypeStruct(h.shape, F32),
        scratch_shapes=[pltpu.VMEM((RET_HEADS, RET_DK, RET_DV), F32),
                        pltpu.VMEM((c, 2 * D_MODEL), BF16)],
        compiler_params=_params("parallel", "arbitrary"),
        name="retention_layer",
    )(chunk_decay, h, cos, sin, g_mix.reshape(1, D_MODEL), w_in, intra, xi, zeta,
      gn_g.reshape(1, 2 * D_MODEL), w_out)


def _gelu(x):
    return 0.5 * x * (1.0 + lax.erf(x * (2.0 ** -0.5)))


def _gmlp_kernel(h_ref, g_ref, w_in_ref, ln_g_ref, ln_b_ref, w_s_ref, b_s_ref, w_out_ref,
                 o_ref, v_ref, prod_ref):
    x = h_ref[...]
    xn = _rms(x, g_ref[...]).astype(BF16)
    gd = GMLP_GROUP_DIM

    total = jnp.zeros((GMLP_TILE, 1), F32)
    for g in range(GMLP_GROUPS):
        vg = _gelu(_dot(xn, w_in_ref[:, GMLP_HALF + g * gd:GMLP_HALF + (g + 1) * gd]))
        v_ref[:, g * gd:(g + 1) * gd] = vg
        total = total + jnp.sum(vg, axis=-1, keepdims=True)
    mu = total * (1.0 / GMLP_HALF)
    sq = jnp.zeros((GMLP_TILE, 1), F32)
    for g in range(GMLP_GROUPS):
        sq = sq + jnp.sum(jnp.square(v_ref[:, g * gd:(g + 1) * gd] - mu), axis=-1, keepdims=True)
    rstd = lax.rsqrt(sq * (1.0 / GMLP_HALF) + GN_EPS)

    row = lax.broadcasted_iota(jnp.int32, (GMLP_CHUNK, GMLP_CHUNK), 0)
    col = lax.broadcasted_iota(jnp.int32, (GMLP_CHUNK, GMLP_CHUNK), 1)
    causal = row >= col
    for g in range(GMLP_GROUPS):
        cols = slice(g * gd, (g + 1) * gd)
        vn = ((v_ref[:, cols] - mu) * rstd * ln_g_ref[:, cols] + ln_b_ref[:, cols]).astype(BF16)
        w_s = jnp.where(causal, w_s_ref[g], 0.0).astype(BF16)
        u = _gelu(_dot(xn, w_in_ref[:, cols]))
        for ch in range(GMLP_TILE // GMLP_CHUNK):
            rows = slice(ch * GMLP_CHUNK, (ch + 1) * GMLP_CHUNK)
            sv = _dot(w_s, vn[rows]) + b_s_ref[g]
            prod_ref[rows, cols] = (u[rows] * sv).astype(BF16)
    o_ref[...] = x + _dot(prod_ref[...], w_out_ref[...])


def _gmlp_layer(h, g_mix, w_in, ln_g, ln_b, w_s, b_s, w_out):
    tile = pl.BlockSpec((None, GMLP_TILE, D_MODEL), lambda b, j: (b, j, 0))
    return pl.pallas_call(
        _gmlp_kernel,
        grid=(BATCH, SEQ // GMLP_TILE),
        in_specs=[tile,
                  _const_spec((1, D_MODEL)),
                  _const_spec(w_in.shape),
                  _const_spec((1, GMLP_HALF)),
                  _const_spec((1, GMLP_HALF)),
                  _const_spec(w_s.shape),
                  _const_spec((GMLP_GROUPS, GMLP_CHUNK, 1)),
                  _const_spec(w_out.shape)],
        out_specs=tile,
        out_shape=jax.ShapeDtypeStruct(h.shape, F32),
        scratch_shapes=[pltpu.VMEM((GMLP_TILE, GMLP_HALF), F32),
                        pltpu.VMEM((GMLP_TILE, GMLP_HALF), BF16)],
        compiler_params=_params("parallel", "parallel"),
        name="gmlp_layer",
    )(h, g_mix.reshape(1, D_MODEL), w_in, ln_g.reshape(1, GMLP_HALF), ln_b.reshape(1, GMLP_HALF),
      w_s, b_s[:, :, None], w_out)


def _conv_kernel(h_ref, g_ref, w_in_ref, k_ref, w_out_ref, o_ref, y_ref):
    @pl.when(pl.program_id(1) == 0)
    def _():
        y_ref[0:SUBLANES, :] = jnp.zeros((SUBLANES, D_MODEL), F32)

    x = h_ref[...]
    xn = _rms(x, g_ref[...]).astype(BF16)
    c_gate = _dot(xn, w_in_ref[:, D_MODEL:2 * D_MODEL])
    hid = _dot(xn, w_in_ref[:, 2 * D_MODEL:])
    y_ref[SUBLANES:, :] = c_gate * hid
    z = k_ref[CONV_WIDTH - 1:CONV_WIDTH, :] * y_ref[SUBLANES:, :]
    for lag in range(1, CONV_WIDTH):
        tap = k_ref[CONV_WIDTH - 1 - lag:CONV_WIDTH - lag, :]
        z = z + tap * y_ref[pl.ds(SUBLANES - lag, CONV_TILE), :]
    y_ref[0:SUBLANES, :] = y_ref[CONV_TILE:, :]
    b_gate = _dot(xn, w_in_ref[:, :D_MODEL])
    o_ref[...] = x + _dot((b_gate * z).astype(BF16), w_out_ref[...])


def _conv_layer(h, g_mix, w_in, conv_k, w_out):
    tile = pl.BlockSpec((None, CONV_TILE, D_MODEL), lambda b, j: (b, j, 0))
    return pl.pallas_call(
        _conv_kernel,
        grid=(BATCH, SEQ // CONV_TILE),
        in_specs=[tile,
                  _const_spec((1, D_MODEL)),
                  _const_spec(w_in.shape),
                  _const_spec((CONV_WIDTH, D_MODEL)),
                  _const_spec(w_out.shape)],
        out_specs=tile,
        out_shape=jax.ShapeDtypeStruct(h.shape, F32),
        scratch_shapes=[pltpu.VMEM((SUBLANES + CONV_TILE, D_MODEL), F32)],
        compiler_params=_params("parallel", "arbitrary"),
        name="conv_layer",
    )(h, g_mix.reshape(1, D_MODEL), w_in, conv_k, w_out)


def _mem_kv_kernel(mem_ref, g_ref, w_kv_ref, kt_ref, v_ref):
    mn = _rms(mem_ref[...], g_ref[...]).astype(BF16)
    kt_ref[...] = _dot(mn, w_kv_ref[:, :D_MODEL]).T.astype(BF16)
    v_ref[...] = _dot(mn, w_kv_ref[:, D_MODEL:]).astype(BF16)


def _mem_kv(mem, norm_mem_g, w_kv):
    return pl.pallas_call(
        _mem_kv_kernel,
        grid=(DEPTH, BATCH),
        in_specs=[pl.BlockSpec((None, MEM_LEN, D_MODEL), lambda i, b: (b, 0, 0)),
                  pl.BlockSpec((None, 1, D_MODEL), lambda i, b: (i, 0, 0)),
                  pl.BlockSpec((None, D_MODEL, 2 * D_MODEL), lambda i, b: (i, 0, 0))],
        out_specs=[pl.BlockSpec((None, None, D_MODEL, MEM_LEN), lambda i, b: (i, b, 0, 0)),
                   pl.BlockSpec((None, None, MEM_LEN, D_MODEL), lambda i, b: (i, b, 0, 0))],
        out_shape=[jax.ShapeDtypeStruct((DEPTH, BATCH, D_MODEL, MEM_LEN), BF16),
                   jax.ShapeDtypeStruct((DEPTH, BATCH, MEM_LEN, D_MODEL), BF16)],
        compiler_params=_params("parallel", "parallel"),
        name="mem_kv",
    )(mem, norm_mem_g.reshape(DEPTH, 1, D_MODEL), w_kv)


def _xattn_kernel(h_ref, g_ref, w_q_ref, kt_ref, v_ref, w_o_ref, o_ref, heads_ref):
    x = h_ref[...]
    xn = _rms(x, g_ref[...]).astype(BF16)
    for hd in range(XA_HEADS):
        cols = slice(hd * XA_DH, (hd + 1) * XA_DH)
        q = _dot(xn, w_q_ref[:, cols]).astype(BF16)
        s = _dot(q, kt_ref[cols, :]) * (XA_DH ** -0.5)
        e = jnp.exp(s - jnp.max(s, axis=-1, keepdims=True))
        p = e / jnp.sum(e, axis=-1, keepdims=True)
        heads_ref[:, cols] = _dot(p.astype(BF16), v_ref[:, cols]).astype(BF16)
    o_ref[...] = x + _dot(heads_ref[...], w_o_ref[...])


def _xattn_layer(h, layer, norm_xa_g, w_q, kt, v, w_o):
    tile = pl.BlockSpec((None, XA_TILE, D_MODEL), lambda b, j: (b, j, 0))
    return pl.pallas_call(
        _xattn_kernel,
        grid=(BATCH, SEQ // XA_TILE),
        in_specs=[tile,
                  _layer_spec((1, D_MODEL), layer),
                  _layer_spec((D_MODEL, D_MODEL), layer),
                  pl.BlockSpec((None, None, D_MODEL, MEM_LEN), lambda b, j: (layer, b, 0, 0)),
                  pl.BlockSpec((None, None, MEM_LEN, D_MODEL), lambda b, j: (layer, b, 0, 0)),
                  _layer_spec((D_MODEL, D_MODEL), layer)],
        out_specs=tile,
        out_shape=jax.ShapeDtypeStruct(h.shape, F32),
        scratch_shapes=[pltpu.VMEM((XA_TILE, D_MODEL), BF16)],
        compiler_params=_params("parallel", "parallel"),
        name="xattn_layer",
    )(h, norm_xa_g.reshape(DEPTH, 1, D_MODEL), w_q, kt, v, w_o)


def _ffn_kernel(h_ref, g_ref, w1_ref, w2_ref, gf_ref, o_ref, act_ref, *, final_norm):
    x = h_ref[...]
    xn = _rms(x, g_ref[...]).astype(BF16)
    for c in range(D_FF // FFN_COLS):
        cols = slice(c * FFN_COLS, (c + 1) * FFN_COLS)
        act_ref[:, cols] = jnp.square(jax.nn.relu(_dot(xn, w1_ref[:, cols]))).astype(BF16)
    y = x + _dot(act_ref[...], w2_ref[...])
    o_ref[...] = _rms(y, gf_ref[...]) if final_norm else y


def _ffn_layer(h, layer, norm_ffn_g, w1, w2, norm_f_g, final_norm):
    tile = pl.BlockSpec((None, FFN_TILE, D_MODEL), lambda b, j: (b, j, 0))
    return pl.pallas_call(
        functools.partial(_ffn_kernel, final_norm=final_norm),
        grid=(BATCH, SEQ // FFN_TILE),
        in_specs=[tile,
                  _layer_spec((1, D_MODEL), layer),
                  _layer_spec((D_MODEL, D_FF), layer),
                  _layer_spec((D_FF, D_MODEL), layer),
                  _const_spec((1, D_MODEL))],
        out_specs=tile,
        out_shape=jax.ShapeDtypeStruct(h.shape, F32),
        scratch_shapes=[pltpu.VMEM((FFN_TILE, D_FF), BF16)],
        compiler_params=_params("parallel", "parallel"),
        name="ffn_layer",
    )(h, norm_ffn_g.reshape(DEPTH, 1, D_MODEL), w1, w2, norm_f_g.reshape(1, D_MODEL))


def kernel(x, mem, positions, norm_mix_g, norm_xa_g, norm_mem_g, xa_w_q, xa_w_kv, xa_w_o,
           norm_ffn_g, ffn_w1, ffn_w2, ret_w_in, ret_gn_g, ret_w_out,
           gmlp_w_in, gmlp_ln_g, gmlp_ln_b, gmlp_w_s, gmlp_b_s, gmlp_w_out,
           conv_w_in, conv_k, conv_w_out, norm_f_g):
    bf = lambda w: w.astype(BF16)
    xa_w_q, xa_w_kv, xa_w_o = bf(xa_w_q), bf(xa_w_kv), bf(xa_w_o)
    ffn_w1, ffn_w2 = bf(ffn_w1), bf(ffn_w2)
    ret_w_in, ret_w_out = bf(ret_w_in), bf(ret_w_out)
    gmlp_w_in, gmlp_w_out = bf(gmlp_w_in), bf(gmlp_w_out)
    conv_w_in, conv_w_out = bf(conv_w_in), bf(conv_w_out)

    cos, sin = _rotary_tables(positions)
    kt, v = _mem_kv(mem, norm_mem_g, xa_w_kv)

    h = x
    for i in range(DEPTH):
        kind, j = i % N_MIXERS, i // N_MIXERS
        if kind == 0:
            h = _retention_layer(h, cos, sin, norm_mix_g[i], ret_w_in[j], ret_gn_g[j],
                                 ret_w_out[j])
        elif kind == 1:
            h = _gmlp_layer(h, norm_mix_g[i], gmlp_w_in[j], gmlp_ln_g[j], gmlp_ln_b[j],
                            gmlp_w_s[j], gmlp_b_s[j], gmlp_w_out[j])
        else:
            h = _conv_layer(h, norm_mix_g[i], conv_w_in[j], conv_k[j], conv_w_out[j])
        h = _xattn_layer(h, i, norm_xa_g, xa_w_q, kt, v, xa_w_o)
        h = _ffn_layer(h, i, norm_ffn_g, ffn_w1, ffn_w2, norm_f_g, final_norm=(i == DEPTH - 1))
    return h
```

```python
import functools

import jax
import jax.numpy as jnp
from jax import lax
from jax.experimental import pallas as pl
from jax.experimental.pallas import tpu as pltpu

D_MODEL = 1024
BATCH = 8
SEQ = 2048
DEPTH = 4
N_MIXERS = 3

RET_HEADS = 4
RET_DK = D_MODEL // RET_HEADS
RET_DV = 2 * D_MODEL // RET_HEADS
ROPE_BASE = 10000.0
ROPE_HALF = RET_DK // 2

GMLP_HALF = 3 * D_MODEL
GMLP_GROUPS = 4
GMLP_GROUP_DIM = GMLP_HALF // GMLP_GROUPS
GMLP_CHUNK = 128

CONV_WIDTH = 3

MEM_LEN = 256
XA_HEADS = 4
XA_DH = D_MODEL // XA_HEADS

D_FF = 4 * D_MODEL

NORM_EPS = 1e-6
GN_EPS = 1e-5

RET_CHUNK = 256
RET_TILE = 512
GMLP_TILE = 256
CONV_TILE = 512
XA_TILE = 512
FFN_TILE = 512
TRIG_TILE = 512
FFN_COLS = 1024
SUBLANES = 8

VMEM_LIMIT_BYTES = 56 * 1024 * 1024

BF16 = jnp.bfloat16
F32 = jnp.float32


def _dot(a, b):
    return jnp.dot(a, b, preferred_element_type=F32)


def _dot_nt(a, b):
    return lax.dot_general(a, b, (((1,), (1,)), ((), ())), preferred_element_type=F32)


def _dot_tn(a, b):
    return lax.dot_general(a, b, (((0,), (0,)), ((), ())), preferred_element_type=F32)


def _rms(x, g):
    return x * lax.rsqrt(jnp.mean(x * x, axis=-1, keepdims=True) + NORM_EPS) * g


def _rms_split(x, g):
    scale = lax.rsqrt(jnp.mean(x * x, axis=-1, keepdims=True) + NORM_EPS)
    return (x * g).astype(BF16), scale


def _const_spec(shape):
    zeros = (0,) * len(shape)
    return pl.BlockSpec(shape, lambda *_: zeros, pipeline_mode=pl.Buffered(1))


def _layer_spec(shape, layer):
    idx = (layer,) + (0,) * len(shape)
    return pl.BlockSpec((None,) + shape, lambda *_: idx, pipeline_mode=pl.Buffered(1))


def _params(*semantics):
    return pltpu.CompilerParams(dimension_semantics=semantics,
                                vmem_limit_bytes=VMEM_LIMIT_BYTES)


def _trig_kernel(pos_ref, inv_ref, cos_ref, sin_ref):
    ang = pos_ref[...].astype(F32) * inv_ref[...]
    cos_ref[...] = jnp.cos(ang)
    sin_ref[...] = jnp.sin(ang)


def _rotary_tables(positions):
    inv_freq = ROPE_BASE ** (-jnp.arange(ROPE_HALF, dtype=F32) / ROPE_HALF)
    out = jax.ShapeDtypeStruct((BATCH, SEQ, ROPE_HALF), F32)
    tile = pl.BlockSpec((None, TRIG_TILE, ROPE_HALF), lambda b, j: (b, j, 0))
    return pl.pallas_call(
        _trig_kernel,
        grid=(BATCH, SEQ // TRIG_TILE),
        in_specs=[pl.BlockSpec((None, TRIG_TILE, 1), lambda b, j: (b, j, 0)),
                  _const_spec((1, ROPE_HALF))],
        out_specs=[tile, tile],
        out_shape=[out, out],
        compiler_params=_params("parallel", "parallel"),
        name="rotary_tables",
    )(positions.reshape(BATCH, SEQ, 1), inv_freq.reshape(1, ROPE_HALF))


def _retention_kernel(decay_ref, h_ref, cos_ref, sin_ref, g_ref, w_in_ref, intra_ref, xi_ref,
                      zeta_ref, gn_ref, w_out_ref, o_ref, state_ref, gated_ref):
    @pl.when(pl.program_id(1) == 0)
    def _():
        state_ref[...] = jnp.zeros_like(state_ref)

    x = h_ref[...]
    xs, scale = _rms_split(x, g_ref[...])

    k_off, v_off, gate_off = D_MODEL, 2 * D_MODEL, 4 * D_MODEL
    q_all = _dot(xs, w_in_ref[:, :k_off]) * scale
    k_all = _dot(xs, w_in_ref[:, k_off:v_off]) * (scale * RET_DK ** -0.5)
    v_all = (_dot(xs, w_in_ref[:, v_off:gate_off]) * scale).astype(BF16)
    gate_all = _dot(xs, w_in_ref[:, gate_off:]) * scale
    for ch in range(RET_TILE // RET_CHUNK):
        rows = slice(ch * RET_CHUNK, (ch + 1) * RET_CHUNK)
        cos = cos_ref[rows, :]
        sin = sin_ref[rows, :]

        def rotate(t):
            t1, t2 = t[:, :ROPE_HALF], t[:, ROPE_HALF:]
            return jnp.concatenate([t1 * cos - t2 * sin, t2 * cos + t1 * sin], axis=-1)

        for hd in range(RET_HEADS):
            dk = slice(hd * RET_DK, (hd + 1) * RET_DK)
            dv = slice(hd * RET_DV, (hd + 1) * RET_DV)
            q = rotate(q_all[rows, dk]).astype(BF16)
            k = rotate(k_all[rows, dk])
            v = v_all[rows, dv]
            state = state_ref[hd]
            s = _dot_nt(q, k.astype(BF16)) * intra_ref[hd]
            o = _dot(s.astype(BF16), v) + _dot(q, state.astype(BF16)) * xi_ref[hd]
            state_ref[hd] = state * decay_ref[hd] + _dot_tn((k * zeta_ref[hd]).astype(BF16), v)
            mu = jnp.mean(o, axis=-1, keepdims=True)
            var = jnp.mean(jnp.square(o - mu), axis=-1, keepdims=True)
            on = (o - mu) * lax.rsqrt(var + GN_EPS) * gn_ref[:, dv]
            gated_ref[rows, dv] = (jax.nn.silu(gate_all[rows, dv]) * on).astype(BF16)
    o_ref[...] = x + _dot(gated_ref[...], w_out_ref[...])


def _retention_layer(h, cos, sin, g_mix, w_in, gn_g, w_out):
    c, t = RET_CHUNK, RET_TILE
    log_g = jnp.log(1.0 - 2.0 ** (-5.0 - jnp.arange(RET_HEADS, dtype=F32)))
    idx = jnp.arange(c, dtype=F32)
    rel = idx[:, None] - idx[None, :]
    intra = jnp.where(rel[None] >= 0, jnp.exp(rel[None] * log_g[:, None, None]), 0.0)
    xi = jnp.exp((idx[None] + 1.0) * log_g[:, None])[..., None]
    zeta = jnp.exp((c - 1.0 - idx[None]) * log_g[:, None])[..., None]
    chunk_decay = jnp.exp(c * log_g)

    tile = pl.BlockSpec((None, t, D_MODEL), lambda b, j: (b, j, 0))
    trig = pl.BlockSpec((None, t, ROPE_HALF), lambda b, j: (b, j, 0))
    return pl.pallas_call(
        _retention_kernel,
        grid=(BATCH, SEQ // t),
        in_specs=[pl.BlockSpec(memory_space=pltpu.SMEM),
                  tile, trig, trig,
                  _const_spec((1, D_MODEL)),
                  _const_spec(w_in.shape),
                  _const_spec((RET_HEADS, c, c)),
                  _const_spec((RET_HEADS, c, 1)),
                  _const_spec((RET_HEADS, c, 1)),
                  _const_spec((1, 2 * D_MODEL)),
                  _const_spec(w_out.shape)],
        out_specs=tile,
        out_shape=jax.ShapeDtypeStruct(h.shape, F32),
        scratch_shapes=[pltpu.VMEM((RET_HEADS, RET_DK, RET_DV), F32),
                        pltpu.VMEM((t, 2 * D_MODEL), BF16)],
        compiler_params=_params("parallel", "arbitrary"),
        name="retention_layer",
    )(chunk_decay, h, cos, sin, g_mix.reshape(1, D_MODEL), w_in, intra, xi, zeta,
      gn_g.reshape(1, 2 * D_MODEL), w_out)


def _gelu(x):
    return 0.5 * x * (1.0 + lax.erf(x * (2.0 ** -0.5)))


def _gmlp_kernel(h_ref, g_ref, w_in_ref, ln_g_ref, ln_b_ref, w_s_ref, b_s_ref, w_out_ref,
                 o_ref, u_ref, vn_ref, prod_ref):
    x = h_ref[...]
    xs, scale = _rms_split(x, g_ref[...])
    gd = GMLP_GROUP_DIM

    v = _gelu(_dot(xs, w_in_ref[:, GMLP_HALF:]) * scale)
    u_ref[...] = _gelu(_dot(xs, w_in_ref[:, :GMLP_HALF]) * scale)
    mu = jnp.mean(v, axis=-1, keepdims=True)
    var = jnp.mean(jnp.square(v - mu), axis=-1, keepdims=True)
    vn_ref[...] = ((v - mu) * lax.rsqrt(var + GN_EPS) * ln_g_ref[...] + ln_b_ref[...]).astype(BF16)

    row = lax.broadcasted_iota(jnp.int32, (GMLP_CHUNK, GMLP_CHUNK), 0)
    col = lax.broadcasted_iota(jnp.int32, (GMLP_CHUNK, GMLP_CHUNK), 1)
    causal = row >= col
    for g in range(GMLP_GROUPS):
        cols = slice(g * gd, (g + 1) * gd)
        w_s = jnp.where(causal, w_s_ref[g], 0.0).astype(BF16)
        for ch in range(GMLP_TILE // GMLP_CHUNK):
            rows = slice(ch * GMLP_CHUNK, (ch + 1) * GMLP_CHUNK)
            sv = _dot(w_s, vn_ref[rows, cols]) + b_s_ref[g]
            prod_ref[rows, cols] = (u_ref[rows, cols] * sv).astype(BF16)
    o_ref[...] = x + _dot(prod_ref[...], w_out_ref[...])


def _gmlp_layer(h, g_mix, w_in, ln_g, ln_b, w_s, b_s, w_out):
    tile = pl.BlockSpec((None, GMLP_TILE, D_MODEL), lambda b, j: (b, j, 0))
    return pl.pallas_call(
        _gmlp_kernel,
        grid=(BATCH, SEQ // GMLP_TILE),
        in_specs=[tile,
                  _const_spec((1, D_MODEL)),
                  _const_spec(w_in.shape),
                  _const_spec((1, GMLP_HALF)),
                  _const_spec((1, GMLP_HALF)),
                  _const_spec(w_s.shape),
                  _const_spec((GMLP_GROUPS, GMLP_CHUNK, 1)),
                  _const_spec(w_out.shape)],
        out_specs=tile,
        out_shape=jax.ShapeDtypeStruct(h.shape, F32),
        scratch_shapes=[pltpu.VMEM((GMLP_TILE, GMLP_HALF), F32),
                        pltpu.VMEM((GMLP_TILE, GMLP_HALF), BF16),
                        pltpu.VMEM((GMLP_TILE, GMLP_HALF), BF16)],
        compiler_params=_params("parallel", "parallel"),
        name="gmlp_layer",
    )(h, g_mix.reshape(1, D_MODEL), w_in, ln_g.reshape(1, GMLP_HALF), ln_b.reshape(1, GMLP_HALF),
      w_s, b_s[:, :, None], w_out)


def _conv_kernel(h_ref, g_ref, w_in_ref, k_ref, w_out_ref, o_ref, y_ref):
    @pl.when(pl.program_id(1) == 0)
    def _():
        y_ref[0:SUBLANES, :] = jnp.zeros((SUBLANES, D_MODEL), F32)

    x = h_ref[...]
    xs, scale = _rms_split(x, g_ref[...])
    c_gate = _dot(xs, w_in_ref[:, D_MODEL:2 * D_MODEL]) * scale
    hid = _dot(xs, w_in_ref[:, 2 * D_MODEL:]) * scale
    y_ref[SUBLANES:, :] = c_gate * hid
    z = k_ref[CONV_WIDTH - 1:CONV_WIDTH, :] * y_ref[SUBLANES:, :]
    for lag in range(1, CONV_WIDTH):
        tap = k_ref[CONV_WIDTH - 1 - lag:CONV_WIDTH - lag, :]
        z = z + tap * y_ref[pl.ds(SUBLANES - lag, CONV_TILE), :]
    y_ref[0:SUBLANES, :] = y_ref[CONV_TILE:, :]
    b_gate = _dot(xs, w_in_ref[:, :D_MODEL]) * scale
    o_ref[...] = x + _dot((b_gate * z).astype(BF16), w_out_ref[...])


def _conv_layer(h, g_mix, w_in, conv_k, w_out):
    tile = pl.BlockSpec((None, CONV_TILE, D_MODEL), lambda b, j: (b, j, 0))
    return pl.pallas_call(
        _conv_kernel,
        grid=(BATCH, SEQ // CONV_TILE),
        in_specs=[tile,
                  _const_spec((1, D_MODEL)),
                  _const_spec(w_in.shape),
                  _const_spec((CONV_WIDTH, D_MODEL)),
                  _const_spec(w_out.shape)],
        out_specs=tile,
        out_shape=jax.ShapeDtypeStruct(h.shape, F32),
        scratch_shapes=[pltpu.VMEM((SUBLANES + CONV_TILE, D_MODEL), F32)],
        compiler_params=_params("parallel", "arbitrary"),
        name="conv_layer",
    )(h, g_mix.reshape(1, D_MODEL), w_in, conv_k, w_out)


def _mem_kv_kernel(mem_ref, g_ref, w_kv_ref, kt_ref, v_ref):
    ms, scale = _rms_split(mem_ref[...], g_ref[...])
    kt_ref[...] = (_dot(ms, w_kv_ref[:, :D_MODEL]) * scale).T.astype(BF16)
    v_ref[...] = (_dot(ms, w_kv_ref[:, D_MODEL:]) * scale).astype(BF16)


def _mem_kv(mem, norm_mem_g, w_kv):
    return pl.pallas_call(
        _mem_kv_kernel,
        grid=(DEPTH, BATCH),
        in_specs=[pl.BlockSpec((None, MEM_LEN, D_MODEL), lambda i, b: (b, 0, 0)),
                  pl.BlockSpec((None, 1, D_MODEL), lambda i, b: (i, 0, 0)),
                  pl.BlockSpec((None, D_MODEL, 2 * D_MODEL), lambda i, b: (i, 0, 0))],
        out_specs=[pl.BlockSpec((None, None, D_MODEL, MEM_LEN), lambda i, b: (i, b, 0, 0)),
                   pl.BlockSpec((None, None, MEM_LEN, D_MODEL), lambda i, b: (i, b, 0, 0))],
        out_shape=[jax.ShapeDtypeStruct((DEPTH, BATCH, D_MODEL, MEM_LEN), BF16),
                   jax.ShapeDtypeStruct((DEPTH, BATCH, MEM_LEN, D_MODEL), BF16)],
        compiler_params=_params("parallel", "parallel"),
        name="mem_kv",
    )(mem, norm_mem_g.reshape(DEPTH, 1, D_MODEL), w_kv)


def _xattn_kernel(h_ref, g_ref, w_q_ref, kt_ref, v_ref, w_o_ref, o_ref, heads_ref):
    x = h_ref[...]
    xs, scale = _rms_split(x, g_ref[...])
    q = (_dot(xs, w_q_ref[...]) * scale).astype(BF16)
    for hd in range(XA_HEADS):
        cols = slice(hd * XA_DH, (hd + 1) * XA_DH)
        s = _dot(q[:, cols], kt_ref[cols, :]) * (XA_DH ** -0.5)
        e = jnp.exp(s - jnp.max(s, axis=-1, keepdims=True))
        p = e / jnp.sum(e, axis=-1, keepdims=True)
        heads_ref[:, cols] = _dot(p.astype(BF16), v_ref[:, cols]).astype(BF16)
    o_ref[...] = x + _dot(heads_ref[...], w_o_ref[...])


def _xattn_layer(h, layer, norm_xa_g, w_q, kt, v, w_o):
    tile = pl.BlockSpec((None, XA_TILE, D_MODEL), lambda b, j: (b, j, 0))
    return pl.pallas_call(
        _xattn_kernel,
        grid=(BATCH, SEQ // XA_TILE),
        in_specs=[tile,
                  _layer_spec((1, D_MODEL), layer),
                  _layer_spec((D_MODEL, D_MODEL), layer),
                  pl.BlockSpec((None, None, D_MODEL, MEM_LEN), lambda b, j: (layer, b, 0, 0)),
                  pl.BlockSpec((None, None, MEM_LEN, D_MODEL), lambda b, j: (layer, b, 0, 0)),
                  _layer_spec((D_MODEL, D_MODEL), layer)],
        out_specs=tile,
        out_shape=jax.ShapeDtypeStruct(h.shape, F32),
        scratch_shapes=[pltpu.VMEM((XA_TILE, D_MODEL), BF16)],
        compiler_params=_params("parallel", "parallel"),
        name="xattn_layer",
    )(h, norm_xa_g.reshape(DEPTH, 1, D_MODEL), w_q, kt, v, w_o)


def _ffn_kernel(h_ref, g_ref, w1_ref, w2_ref, gf_ref, o_ref, act_ref, *, final_norm):
    x = h_ref[...]
    xs, scale = _rms_split(x, g_ref[...])
    for c in range(D_FF // FFN_COLS):
        cols = slice(c * FFN_COLS, (c + 1) * FFN_COLS)
        act_ref[:, cols] = jnp.square(jax.nn.relu(_dot(xs, w1_ref[:, cols]) * scale)).astype(BF16)
    y = x + _dot(act_ref[...], w2_ref[...])
    o_ref[...] = _rms(y, gf_ref[...]) if final_norm else y


def _ffn_layer(h, layer, norm_ffn_g, w1, w2, norm_f_g, final_norm):
    tile = pl.BlockSpec((None, FFN_TILE, D_MODEL), lambda b, j: (b, j, 0))
    return pl.pallas_call(
        functools.partial(_ffn_kernel, final_norm=final_norm),
        grid=(BATCH, SEQ // FFN_TILE),
        in_specs=[tile,
                  _layer_spec((1, D_MODEL), layer),
                  _layer_spec((D_MODEL, D_FF), layer),
                  _layer_spec((D_FF, D_MODEL), layer),
                  _const_spec((1, D_MODEL))],
        out_specs=tile,
        out_shape=jax.ShapeDtypeStruct(h.shape, F32),
        scratch_shapes=[pltpu.VMEM((FFN_TILE, D_FF), BF16)],
        compiler_params=_params("parallel", "parallel"),
        name="ffn_layer",
    )(h, norm_ffn_g.reshape(DEPTH, 1, D_MODEL), w1, w2, norm_f_g.reshape(1, D_MODEL))


def kernel(x, mem, positions, norm_mix_g, norm_xa_g, norm_mem_g, xa_w_q, xa_w_kv, xa_w_o,
           norm_ffn_g, ffn_w1, ffn_w2, ret_w_in, ret_gn_g, ret_w_out,
           gmlp_w_in, gmlp_ln_g, gmlp_ln_b, gmlp_w_s, gmlp_b_s, gmlp_w_out,
           conv_w_in, conv_k, conv_w_out, norm_f_g):
    bf = lambda w: w.astype(BF16)
    xa_w_q, xa_w_kv, xa_w_o = bf(xa_w_q), bf(xa_w_kv), bf(xa_w_o)
    ffn_w1, ffn_w2 = bf(ffn_w1), bf(ffn_w2)
    ret_w_in, ret_w_out = bf(ret_w_in), bf(ret_w_out)
    gmlp_w_in, gmlp_w_out = bf(gmlp_w_in), bf(gmlp_w_out)
    conv_w_in, conv_w_out = bf(conv_w_in), bf(conv_w_out)

    cos, sin = _rotary_tables(positions)
    kt, v = _mem_kv(mem, norm_mem_g, xa_w_kv)

    h = x
    for i in range(DEPTH):
        kind, j = i % N_MIXERS, i // N_MIXERS
        if kind == 0:
            h = _retention_layer(h, cos, sin, norm_mix_g[i], ret_w_in[j], ret_gn_g[j],
                                 ret_w_out[j])
        elif kind == 1:
            h = _gmlp_layer(h, norm_mix_g[i], gmlp_w_in[j], gmlp_ln_g[j], gmlp_ln_b[j],
                            gmlp_w_s[j], gmlp_b_s[j], gmlp_w_out[j])
        else:
            h = _conv_layer(h, norm_mix_g[i], conv_w_in[j], conv_k[j], conv_w_out[j])
        h = _xattn_layer(h, i, norm_xa_g, xa_w_q, kt, v, xa_w_o)
        h = _ffn_layer(h, i, norm_ffn_g, ffn_w1, ffn_w2, norm_f_g, final_norm=(i == DEPTH - 1))
    return h
```

```python
import functools

import jax
import jax.numpy as jnp
from jax import lax
from jax.experimental import pallas as pl
from jax.experimental.pallas import tpu as pltpu

D_MODEL = 1024
BATCH = 8
SEQ = 2048
DEPTH = 4
N_MIXERS = 3

RET_HEADS = 4
RET_DK = D_MODEL // RET_HEADS
RET_DV = 2 * D_MODEL // RET_HEADS
RET_IN = 6 * D_MODEL
ROPE_BASE = 10000.0
ROPE_HALF = RET_DK // 2

GMLP_HALF = 3 * D_MODEL
GMLP_GROUPS = 4
GMLP_GROUP_DIM = GMLP_HALF // GMLP_GROUPS
GMLP_CHUNK = 128

CONV_WIDTH = 3

MEM_LEN = 256
XA_HEADS = 4
XA_DH = D_MODEL // XA_HEADS

D_FF = 4 * D_MODEL

NORM_EPS = 1e-6
GN_EPS = 1e-5

RET_CHUNK = 256
RET_TILE = 512
GMLP_TILE = 512
GMLP_SUB = 256
CONV_TILE = 512
XF_TILE = 512
XA_SUB = 256
TRIG_TILE = 512
FFN_COLS = 1024
OUT_COLS = 512
SUBLANES = 8

VMEM_LIMIT_BYTES = 56 * 1024 * 1024

BF16 = jnp.bfloat16
F32 = jnp.float32


def _dot(a, b):
    return jnp.dot(a, b, preferred_element_type=F32)


def _dot_nt(a, b):
    return lax.dot_general(a, b, (((1,), (1,)), ((), ())), preferred_element_type=F32)


def _dot_tn(a, b):
    return lax.dot_general(a, b, (((0,), (0,)), ((), ())), preferred_element_type=F32)


def _rms(x, g):
    return x * lax.rsqrt(jnp.mean(x * x, axis=-1, keepdims=True) + NORM_EPS) * g


def _rms_split(x, g):
    scale = lax.rsqrt(jnp.mean(x * x, axis=-1, keepdims=True) + NORM_EPS)
    return (x * g).astype(BF16), scale


def _const_spec(shape):
    zeros = (0,) * len(shape)
    return pl.BlockSpec(shape, lambda *_: zeros, pipeline_mode=pl.Buffered(1))


def _layer_spec(shape, layer):
    idx = (layer,) + (0,) * len(shape)
    return pl.BlockSpec((None,) + shape, lambda *_: idx, pipeline_mode=pl.Buffered(1))


def _params(*semantics):
    return pltpu.CompilerParams(dimension_semantics=semantics,
                                vmem_limit_bytes=VMEM_LIMIT_BYTES)


def _trig_kernel(pos_ref, inv_ref, cos_ref, sin_ref):
    ang = pos_ref[...].astype(F32) * inv_ref[...]
    cos_ref[...] = jnp.cos(ang)
    sin_ref[...] = jnp.sin(ang)


def _rotary_tables(positions):
    inv_freq = ROPE_BASE ** (-jnp.arange(ROPE_HALF, dtype=F32) / ROPE_HALF)
    out = jax.ShapeDtypeStruct((BATCH, SEQ, ROPE_HALF), F32)
    tile = pl.BlockSpec((None, TRIG_TILE, ROPE_HALF), lambda b, j: (b, j, 0))
    return pl.pallas_call(
        _trig_kernel,
        grid=(BATCH, SEQ // TRIG_TILE),
        in_specs=[pl.BlockSpec((None, TRIG_TILE, 1), lambda b, j: (b, j, 0)),
                  _const_spec((1, ROPE_HALF))],
        out_specs=[tile, tile],
        out_shape=[out, out],
        compiler_params=_params("parallel", "parallel"),
        name="rotary_tables",
    )(positions.reshape(BATCH, SEQ, 1), inv_freq.reshape(1, ROPE_HALF))


def _retention_kernel(decay_ref, h_ref, cos_ref, sin_ref, g_ref, w_in_ref, intra_ref, xi_ref,
                      zeta_ref, gn_ref, w_out_ref, o_ref, state_ref, gated_ref):
    @pl.when(pl.program_id(1) == 0)
    def _():
        state_ref[...] = jnp.zeros_like(state_ref)

    x = h_ref[...]
    xs, scale = _rms_split(x, g_ref[...])

    k_off, v_off, gate_off = D_MODEL, 2 * D_MODEL, 4 * D_MODEL
    q_all = _dot(xs, w_in_ref[:, :k_off]) * scale
    k_all = _dot(xs, w_in_ref[:, k_off:v_off]) * (scale * RET_DK ** -0.5)
    v_all = (_dot(xs, w_in_ref[:, v_off:gate_off]) * scale).astype(BF16)

    def recur(rows, hd, gate):
        cos = cos_ref[rows, :]
        sin = sin_ref[rows, :]

        def rotate(t):
            t1, t2 = t[:, :ROPE_HALF], t[:, ROPE_HALF:]
            return jnp.concatenate([t1 * cos - t2 * sin, t2 * cos + t1 * sin], axis=-1)

        dk = slice(hd * RET_DK, (hd + 1) * RET_DK)
        dv = slice(hd * RET_DV, (hd + 1) * RET_DV)
        q = rotate(q_all[rows, dk]).astype(BF16)
        k = rotate(k_all[rows, dk])
        v = v_all[rows, dv]
        state = state_ref[hd]
        s = _dot_nt(q, k.astype(BF16)) * intra_ref[hd]
        o = _dot(s.astype(BF16), v) + _dot(q, state.astype(BF16)) * xi_ref[hd]
        state_ref[hd] = state * decay_ref[hd] + _dot_tn((k * zeta_ref[hd]).astype(BF16), v)
        mu = jnp.mean(o, axis=-1, keepdims=True)
        var = jnp.mean(jnp.square(o - mu), axis=-1, keepdims=True)
        on = (o - mu) * lax.rsqrt(var + GN_EPS) * gn_ref[:, dv]
        gated_ref[rows, dv] = (jax.nn.silu(gate) * on).astype(BF16)

    chunks = [slice(c * RET_CHUNK, (c + 1) * RET_CHUNK) for c in range(RET_TILE // RET_CHUNK)]
    out_cols = [slice(n * OUT_COLS, (n + 1) * OUT_COLS) for n in range(D_MODEL // OUT_COLS)]
    gates = []
    for hd in range(RET_HEADS):
        lo = gate_off + hd * RET_DV
        gates.append(_dot(xs, w_in_ref[:, lo:lo + RET_DV]) * scale)
        recur(chunks[0], hd, gates[hd][chunks[0]])
    for prev, rows in zip(chunks[:-1], chunks[1:]):
        pieces = []
        for hd in range(RET_HEADS):
            recur(rows, hd, gates[hd][rows])
            if hd < len(out_cols):
                pieces.append(_dot(gated_ref[prev, :], w_out_ref[:, out_cols[hd]]))
        o_ref[prev, :] = x[prev] + jnp.concatenate(pieces, axis=-1)
    last = chunks[-1]
    o_ref[last, :] = x[last] + _dot(gated_ref[last, :], w_out_ref[...])


def _retention_layer(h, cos, sin, layer, j, norm_mix_g, w_in, gn_g, w_out):
    c, t = RET_CHUNK, RET_TILE
    log_g = jnp.log(1.0 - 2.0 ** (-5.0 - jnp.arange(RET_HEADS, dtype=F32)))
    idx = jnp.arange(c, dtype=F32)
    rel = idx[:, None] - idx[None, :]
    intra = jnp.where(rel[None] >= 0, jnp.exp(rel[None] * log_g[:, None, None]), 0.0)
    xi = jnp.exp((idx[None] + 1.0) * log_g[:, None])[..., None]
    zeta = jnp.exp((c - 1.0 - idx[None]) * log_g[:, None])[..., None]
    chunk_decay = jnp.exp(c * log_g)

    tile = pl.BlockSpec((None, t, D_MODEL), lambda b, s: (b, s, 0))
    trig = pl.BlockSpec((None, t, ROPE_HALF), lambda b, s: (b, s, 0))
    return pl.pallas_call(
        _retention_kernel,
        grid=(BATCH, SEQ // t),
        in_specs=[pl.BlockSpec(memory_space=pltpu.SMEM),
                  tile, trig, trig,
                  _layer_spec((1, D_MODEL), layer),
                  _layer_spec((D_MODEL, RET_IN), j),
                  _const_spec((RET_HEADS, c, c)),
                  _const_spec((RET_HEADS, c, 1)),
                  _const_spec((RET_HEADS, c, 1)),
                  _layer_spec((1, 2 * D_MODEL), j),
                  _layer_spec((2 * D_MODEL, D_MODEL), j)],
        out_specs=tile,
        out_shape=jax.ShapeDtypeStruct(h.shape, F32),
        scratch_shapes=[pltpu.VMEM((RET_HEADS, RET_DK, RET_DV), F32),
                        pltpu.VMEM((t, 2 * D_MODEL), BF16)],
        compiler_params=_params("parallel", "arbitrary"),
        name="retention_layer",
    )(chunk_decay, h, cos, sin, norm_mix_g.reshape(DEPTH, 1, D_MODEL), w_in, intra, xi, zeta,
      gn_g.reshape(-1, 1, 2 * D_MODEL), w_out)


def _gelu(x):
    return 0.5 * x * (1.0 + lax.erf(x * (2.0 ** -0.5)))


def _gmlp_kernel(h_ref, g_ref, w_in_ref, ln_g_ref, ln_b_ref, w_s_ref, b_s_ref, w_out_ref,
                 o_ref, v_ref, prod_ref):
    gd = GMLP_GROUP_DIM
    groups = [slice(g * gd, (g + 1) * gd) for g in range(GMLP_GROUPS)]
    row = lax.broadcasted_iota(jnp.int32, (GMLP_CHUNK, GMLP_CHUNK), 0)
    col = lax.broadcasted_iota(jnp.int32, (GMLP_CHUNK, GMLP_CHUNK), 1)
    w_s = [jnp.where(row >= col, w_s_ref[g], 0.0).astype(BF16) for g in range(GMLP_GROUPS)]

    def gate_values(rows):
        x = h_ref[rows, :]
        xs, scale = _rms_split(x, g_ref[...])
        total = jnp.zeros((GMLP_SUB, 1), F32)
        for cols in groups:
            lo = GMLP_HALF + cols.start
            vg = _gelu(_dot(xs, w_in_ref[:, lo:lo + gd]) * scale)
            v_ref[rows, cols] = vg
            total = total + jnp.sum(vg, axis=-1, keepdims=True)
        mu = total * (1.0 / GMLP_HALF)
        sq = jnp.zeros((GMLP_SUB, 1), F32)
        for cols in groups:
            sq = sq + jnp.sum(jnp.square(v_ref[rows, cols] - mu), axis=-1, keepdims=True)
        return x, xs, scale, mu, lax.rsqrt(sq * (1.0 / GMLP_HALF) + GN_EPS)

    def mix(rows, xs, scale, mu, rstd):
        for g, cols in enumerate(groups):
            vn = ((v_ref[rows, cols] - mu) * rstd * ln_g_ref[:, cols] + ln_b_ref[:, cols]).astype(BF16)
            u = _gelu(_dot(xs, w_in_ref[:, cols]) * scale)
            for ch in range(GMLP_SUB // GMLP_CHUNK):
                sub = slice(ch * GMLP_CHUNK, (ch + 1) * GMLP_CHUNK)
                dst = slice(rows.start + sub.start, rows.start + sub.stop)
                sv = _dot(w_s[g], vn[sub]) + b_s_ref[g]
                prod_ref[dst, cols] = (u[sub] * sv).astype(BF16)

    def project(rows, x):
        o_ref[rows, :] = x + _dot(prod_ref[rows, :], w_out_ref[...])

    subs = [slice(i * GMLP_SUB, (i + 1) * GMLP_SUB) for i in range(GMLP_TILE // GMLP_SUB)]
    stats = [gate_values(rows) for rows in subs]
    for rows, (x, xs, scale, mu, rstd) in zip(subs, stats):
        mix(rows, xs, scale, mu, rstd)
        project(rows, x)


def _gmlp_layer(h, g_mix, w_in, ln_g, ln_b, w_s, b_s, w_out):
    tile = pl.BlockSpec((None, GMLP_TILE, D_MODEL), lambda b, j: (b, j, 0))
    return pl.pallas_call(
        _gmlp_kernel,
        grid=(BATCH, SEQ // GMLP_TILE),
        in_specs=[tile,
                  _const_spec((1, D_MODEL)),
                  _const_spec(w_in.shape),
                  _const_spec((1, GMLP_HALF)),
                  _const_spec((1, GMLP_HALF)),
                  _const_spec(w_s.shape),
                  _const_spec((GMLP_GROUPS, GMLP_CHUNK, 1)),
                  _const_spec(w_out.shape)],
        out_specs=tile,
        out_shape=jax.ShapeDtypeStruct(h.shape, F32),
        scratch_shapes=[pltpu.VMEM((GMLP_TILE, GMLP_HALF), F32),
                        pltpu.VMEM((GMLP_TILE, GMLP_HALF), BF16)],
        compiler_params=_params("parallel", "parallel"),
        name="gmlp_layer",
    )(h, g_mix.reshape(1, D_MODEL), w_in, ln_g.reshape(1, GMLP_HALF), ln_b.reshape(1, GMLP_HALF),
      w_s, b_s[:, :, None], w_out)


def _conv_kernel(h_ref, g_ref, w_in_ref, k_ref, w_out_ref, o_ref, y_ref):
    @pl.when(pl.program_id(1) == 0)
    def _():
        y_ref[0:SUBLANES, :] = jnp.zeros((SUBLANES, D_MODEL), F32)

    x = h_ref[...]
    xs, scale = _rms_split(x, g_ref[...])
    c_gate = _dot(xs, w_in_ref[:, D_MODEL:2 * D_MODEL]) * scale
    hid = _dot(xs, w_in_ref[:, 2 * D_MODEL:]) * scale
    y_ref[SUBLANES:, :] = c_gate * hid
    z = k_ref[CONV_WIDTH - 1:CONV_WIDTH, :] * y_ref[SUBLANES:, :]
    for lag in range(1, CONV_WIDTH):
        tap = k_ref[CONV_WIDTH - 1 - lag:CONV_WIDTH - lag, :]
        z = z + tap * y_ref[pl.ds(SUBLANES - lag, CONV_TILE), :]
    y_ref[0:SUBLANES, :] = y_ref[CONV_TILE:, :]
    b_gate = _dot(xs, w_in_ref[:, :D_MODEL]) * scale
    o_ref[...] = x + _dot((b_gate * z).astype(BF16), w_out_ref[...])


def _conv_layer(h, g_mix, w_in, conv_k, w_out):
    tile = pl.BlockSpec((None, CONV_TILE, D_MODEL), lambda b, j: (b, j, 0))
    return pl.pallas_call(
        _conv_kernel,
        grid=(BATCH, SEQ // CONV_TILE),
        in_specs=[tile,
                  _const_spec((1, D_MODEL)),
                  _const_spec(w_in.shape),
                  _const_spec((CONV_WIDTH, D_MODEL)),
                  _const_spec(w_out.shape)],
        out_specs=tile,
        out_shape=jax.ShapeDtypeStruct(h.shape, F32),
        scratch_shapes=[pltpu.VMEM((SUBLANES + CONV_TILE, D_MODEL), F32)],
        compiler_params=_params("parallel", "arbitrary"),
        name="conv_layer",
    )(h, g_mix.reshape(1, D_MODEL), w_in, conv_k, w_out)


def _mem_kv_kernel(mem_ref, g_ref, w_kv_ref, kt_ref, v_ref):
    ms, scale = _rms_split(mem_ref[...], g_ref[...])
    kt_ref[...] = (_dot(ms, w_kv_ref[:, :D_MODEL]) * scale).T.astype(BF16)
    v_ref[...] = (_dot(ms, w_kv_ref[:, D_MODEL:]) * scale).astype(BF16)


def _mem_kv(mem, norm_mem_g, w_kv):
    return pl.pallas_call(
        _mem_kv_kernel,
        grid=(DEPTH, BATCH),
        in_specs=[pl.BlockSpec((None, MEM_LEN, D_MODEL), lambda i, b: (b, 0, 0)),
                  pl.BlockSpec((None, 1, D_MODEL), lambda i, b: (i, 0, 0)),
                  pl.BlockSpec((None, D_MODEL, 2 * D_MODEL), lambda i, b: (i, 0, 0))],
        out_specs=[pl.BlockSpec((None, None, D_MODEL, MEM_LEN), lambda i, b: (i, b, 0, 0)),
                   pl.BlockSpec((None, None, MEM_LEN, D_MODEL), lambda i, b: (i, b, 0, 0))],
        out_shape=[jax.ShapeDtypeStruct((DEPTH, BATCH, D_MODEL, MEM_LEN), BF16),
                   jax.ShapeDtypeStruct((DEPTH, BATCH, MEM_LEN, D_MODEL), BF16)],
        compiler_params=_params("parallel", "parallel"),
        name="mem_kv",
    )(mem, norm_mem_g.reshape(DEPTH, 1, D_MODEL), w_kv)


def _xattn_ffn_kernel(h_ref, gx_ref, w_q_ref, kt_ref, v_ref, w_o_ref, gf_ref, w1_ref, w2_ref,
                      gfin_ref, o_ref, heads_ref, mid_ref, act_ref, *, final_norm):
    head_cols = [slice(hd * XA_DH, (hd + 1) * XA_DH) for hd in range(XA_HEADS)]

    def query(rows):
        x = h_ref[rows, :]
        xs, scale = _rms_split(x, gx_ref[...])
        return x, (_dot(xs, w_q_ref[...]) * scale).astype(BF16)

    def probs(q):
        out = []
        for cols in head_cols:
            s = _dot(q[:, cols], kt_ref[cols, :]) * (XA_DH ** -0.5)
            e = jnp.exp(s - jnp.max(s, axis=-1, keepdims=True))
            out.append((e / jnp.sum(e, axis=-1, keepdims=True)).astype(BF16))
        return out

    def attend(x, p):
        heads = [_dot(ph, v_ref[:, cols]).astype(BF16) for ph, cols in zip(p, head_cols)]
        return x + _dot(jnp.concatenate(heads, axis=-1), w_o_ref[...])

    def hidden(y):
        ys, scale = _rms_split(y, gf_ref[...])
        acts = []
        for c in range(D_FF // FFN_COLS):
            cols = slice(c * FFN_COLS, (c + 1) * FFN_COLS)
            acts.append(jnp.square(jax.nn.relu(_dot(ys, w1_ref[:, cols]) * scale)).astype(BF16))
        return jnp.concatenate(acts, axis=-1)

    def finish(rows, y, act):
        out = y + _dot(act, w2_ref[...])
        o_ref[rows, :] = _rms(out, gfin_ref[...]) if final_norm else out

    rows_a, rows_b = (slice(i * XA_SUB, (i + 1) * XA_SUB) for i in range(XF_TILE // XA_SUB))
    x_a, q_a = query(rows_a)
    p_a = probs(q_a)
    x_b, q_b = query(rows_b)
    y_a = attend(x_a, p_a)
    p_b = probs(q_b)
    act_a = hidden(y_a)
    y_b = attend(x_b, p_b)
    finish(rows_a, y_a, act_a)
    finish(rows_b, y_b, hidden(y_b))


def _xattn_ffn_layer(h, layer, norm_xa_g, w_q, kt, v, w_o, norm_ffn_g, w1, w2, norm_f_g,
                     final_norm):
    tile = pl.BlockSpec((None, XF_TILE, D_MODEL), lambda b, j: (b, j, 0))
    return pl.pallas_call(
        functools.partial(_xattn_ffn_kernel, final_norm=final_norm),
        grid=(BATCH, SEQ // XF_TILE),
        in_specs=[tile,
                  _layer_spec((1, D_MODEL), layer),
                  _layer_spec((D_MODEL, D_MODEL), layer),
                  pl.BlockSpec((None, None, D_MODEL, MEM_LEN), lambda b, j: (layer, b, 0, 0)),
                  pl.BlockSpec((None, None, MEM_LEN, D_MODEL), lambda b, j: (layer, b, 0, 0)),
                  _layer_spec((D_MODEL, D_MODEL), layer),
                  _layer_spec((1, D_MODEL), layer),
                  _layer_spec((D_MODEL, D_FF), layer),
                  _layer_spec((D_FF, D_MODEL), layer),
                  _const_spec((1, D_MODEL))],
        out_specs=tile,
        out_shape=jax.ShapeDtypeStruct(h.shape, F32),
        scratch_shapes=[pltpu.VMEM((XF_TILE, D_MODEL), BF16),
                        pltpu.VMEM((XF_TILE, D_MODEL), F32),
                        pltpu.VMEM((XF_TILE, D_FF), BF16)],
        compiler_params=_params("parallel", "parallel"),
        name="xattn_ffn_layer",
    )(h, norm_xa_g.reshape(DEPTH, 1, D_MODEL), w_q, kt, v, w_o,
      norm_ffn_g.reshape(DEPTH, 1, D_MODEL), w1, w2, norm_f_g.reshape(1, D_MODEL))


def kernel(x, mem, positions, norm_mix_g, norm_xa_g, norm_mem_g, xa_w_q, xa_w_kv, xa_w_o,
           norm_ffn_g, ffn_w1, ffn_w2, ret_w_in, ret_gn_g, ret_w_out,
           gmlp_w_in, gmlp_ln_g, gmlp_ln_b, gmlp_w_s, gmlp_b_s, gmlp_w_out,
           conv_w_in, conv_k, conv_w_out, norm_f_g):
    bf = lambda w: w.astype(BF16)
    xa_w_q, xa_w_kv, xa_w_o = bf(xa_w_q), bf(xa_w_kv), bf(xa_w_o)
    ffn_w1, ffn_w2 = bf(ffn_w1), bf(ffn_w2)
    ret_w_in, ret_w_out = bf(ret_w_in), bf(ret_w_out)
    gmlp_w_in, gmlp_w_out = bf(gmlp_w_in), bf(gmlp_w_out)
    conv_w_in, conv_w_out = bf(conv_w_in), bf(conv_w_out)

    cos, sin = _rotary_tables(positions)
    kt, v = _mem_kv(mem, norm_mem_g, xa_w_kv)

    h = x
    for i in range(DEPTH):
        kind, j = i % N_MIXERS, i // N_MIXERS
        if kind == 0:
            h = _retention_layer(h, cos, sin, i, j, norm_mix_g, ret_w_in, ret_gn_g, ret_w_out)
        elif kind == 1:
            h = _gmlp_layer(h, norm_mix_g[i], gmlp_w_in[j], gmlp_ln_g[j], gmlp_ln_b[j],
                            gmlp_w_s[j], gmlp_b_s[j], gmlp_w_out[j])
        else:
            h = _conv_layer(h, norm_mix_g[i], conv_w_in[j], conv_k[j], conv_w_out[j])
        h = _xattn_ffn_layer(h, i, norm_xa_g, xa_w_q, kt, v, xa_w_o, norm_ffn_g, ffn_w1, ffn_w2,
                             norm_f_g, final_norm=(i == DEPTH - 1))
    return h
```

```python
import functools

import jax
import jax.numpy as jnp
from jax import lax
from jax.experimental import pallas as pl
from jax.experimental.pallas import tpu as pltpu

D_MODEL = 1024
BATCH = 8
SEQ = 2048
DEPTH = 4
N_MIXERS = 3

RET_HEADS = 4
RET_DK = D_MODEL // RET_HEADS
RET_DV = 2 * D_MODEL // RET_HEADS
RET_IN = 6 * D_MODEL
ROPE_BASE = 10000.0
ROPE_HALF = RET_DK // 2

GMLP_HALF = 3 * D_MODEL
GMLP_GROUPS = 4
GMLP_GROUP_DIM = GMLP_HALF // GMLP_GROUPS
GMLP_CHUNK = 128

CONV_WIDTH = 3

MEM_LEN = 256
XA_HEADS = 4
XA_DH = D_MODEL // XA_HEADS

D_FF = 4 * D_MODEL

NORM_EPS = 1e-6
GN_EPS = 1e-5

RET_CHUNK = 256
RET_TILE = 512
GMLP_TILE = 512
GMLP_SUB = 256
CONV_TILE = 512
XF_TILE = 512
XA_SUB = 256
TRIG_TILE = 512
FFN_COLS = 1024
OUT_COLS = 512
SUBLANES = 8
STAGE_ROWS, STAGE_COLS = 512, 1024

VMEM_LIMIT_BYTES = 56 * 1024 * 1024

BF16 = jnp.bfloat16
F32 = jnp.float32


def _dot(a, b):
    return jnp.dot(a, b, preferred_element_type=F32)


def _dot_nt(a, b):
    return lax.dot_general(a, b, (((1,), (1,)), ((), ())), preferred_element_type=F32)


def _dot_tn(a, b):
    return lax.dot_general(a, b, (((0,), (0,)), ((), ())), preferred_element_type=F32)


def _rms(x, g):
    return x * lax.rsqrt(jnp.mean(x * x, axis=-1, keepdims=True) + NORM_EPS) * g


def _rms_split(x, g):
    scale = lax.rsqrt(jnp.mean(x * x, axis=-1, keepdims=True) + NORM_EPS)
    return (x * g).astype(BF16), scale


def _const_spec(shape):
    zeros = (0,) * len(shape)
    return pl.BlockSpec(shape, lambda *_: zeros, pipeline_mode=pl.Buffered(1))


def _layer_spec(shape, layer):
    idx = (layer,) + (0,) * len(shape)
    return pl.BlockSpec((None,) + shape, lambda *_: idx, pipeline_mode=pl.Buffered(1))


def _params(*semantics):
    return pltpu.CompilerParams(dimension_semantics=semantics,
                                vmem_limit_bytes=VMEM_LIMIT_BYTES)


_SEQUENTIAL = ("arbitrary", "arbitrary")
_HBM = pl.BlockSpec(memory_space=pl.ANY)
_STAGE_SCRATCH = [pltpu.VMEM((2, STAGE_ROWS, STAGE_COLS), F32), pltpu.SemaphoreType.DMA((2,))]


def _weight_scratch(*shapes):
    return [pltpu.VMEM(shape, BF16) for shape in shapes]


def _first_step():
    return jnp.logical_and(pl.program_id(0) == 0, pl.program_id(1) == 0)


def _stage_weights(jobs, stage_ref, sem_ref):
    blocks = []
    for src, dst in jobs:
        rows, cols = dst.shape
        assert rows % STAGE_ROWS == 0 and cols % STAGE_COLS == 0, dst.shape
        blocks += [(src, dst, r, c) for r in range(0, rows, STAGE_ROWS)
                   for c in range(0, cols, STAGE_COLS)]

    def fetch(n):
        src, _, r, c = blocks[n]
        return pltpu.make_async_copy(src.at[pl.ds(r, STAGE_ROWS), pl.ds(c, STAGE_COLS)],
                                     stage_ref.at[n % 2], sem_ref.at[n % 2])

    fetch(0).start()
    for n, (_, dst, r, c) in enumerate(blocks):
        if n + 1 < len(blocks):
            fetch(n + 1).start()
        fetch(n).wait()
        dst[r:r + STAGE_ROWS, c:c + STAGE_COLS] = stage_ref[n % 2].astype(BF16)


def _trig_kernel(pos_ref, inv_ref, cos_ref, sin_ref):
    ang = pos_ref[...].astype(F32) * inv_ref[...]
    cos_ref[...] = jnp.cos(ang)
    sin_ref[...] = jnp.sin(ang)


def _rotary_tables(positions):
    inv_freq = ROPE_BASE ** (-jnp.arange(ROPE_HALF, dtype=F32) / ROPE_HALF)
    out = jax.ShapeDtypeStruct((BATCH, SEQ, ROPE_HALF), F32)
    tile = pl.BlockSpec((None, TRIG_TILE, ROPE_HALF), lambda b, j: (b, j, 0))
    return pl.pallas_call(
        _trig_kernel,
        grid=(BATCH, SEQ // TRIG_TILE),
        in_specs=[pl.BlockSpec((None, TRIG_TILE, 1), lambda b, j: (b, j, 0)),
                  _const_spec((1, ROPE_HALF))],
        out_specs=[tile, tile],
        out_shape=[out, out],
        compiler_params=_params("parallel", "parallel"),
        name="rotary_tables",
    )(positions.reshape(BATCH, SEQ, 1), inv_freq.reshape(1, ROPE_HALF))


def _retention_kernel(decay_ref, h_ref, cos_ref, sin_ref, g_ref, w_in_hbm, intra_ref, xi_ref,
                      zeta_ref, gn_ref, w_out_hbm, o_ref, state_ref, gated_ref, w_in_ref,
                      w_out_ref, stage_ref, sem_ref, *, j):
    @pl.when(_first_step())
    def _():
        _stage_weights([(w_in_hbm.at[j], w_in_ref), (w_out_hbm.at[j], w_out_ref)],
                       stage_ref, sem_ref)

    @pl.when(pl.program_id(1) == 0)
    def _():
        state_ref[...] = jnp.zeros_like(state_ref)

    x = h_ref[...]
    xs, scale = _rms_split(x, g_ref[...])

    k_off, v_off, gate_off = D_MODEL, 2 * D_MODEL, 4 * D_MODEL
    q_all = _dot(xs, w_in_ref[:, :k_off]) * scale
    k_all = _dot(xs, w_in_ref[:, k_off:v_off]) * (scale * RET_DK ** -0.5)
    v_all = (_dot(xs, w_in_ref[:, v_off:gate_off]) * scale).astype(BF16)

    def recur(rows, hd, gate):
        cos = cos_ref[rows, :]
        sin = sin_ref[rows, :]

        def rotate(t):
            t1, t2 = t[:, :ROPE_HALF], t[:, ROPE_HALF:]
            return jnp.concatenate([t1 * cos - t2 * sin, t2 * cos + t1 * sin], axis=-1)

        dk = slice(hd * RET_DK, (hd + 1) * RET_DK)
        dv = slice(hd * RET_DV, (hd + 1) * RET_DV)
        q = rotate(q_all[rows, dk]).astype(BF16)
        k = rotate(k_all[rows, dk])
        v = v_all[rows, dv]
        state = state_ref[hd]
        s = _dot_nt(q, k.astype(BF16)) * intra_ref[hd]
        o = _dot(s.astype(BF16), v) + _dot(q, state.astype(BF16)) * xi_ref[hd]
        state_ref[hd] = state * decay_ref[hd] + _dot_tn((k * zeta_ref[hd]).astype(BF16), v)
        mu = jnp.mean(o, axis=-1, keepdims=True)
        var = jnp.mean(jnp.square(o - mu), axis=-1, keepdims=True)
        on = (o - mu) * lax.rsqrt(var + GN_EPS) * gn_ref[:, dv]
        gated_ref[rows, dv] = (jax.nn.silu(gate) * on).astype(BF16)

    chunks = [slice(c * RET_CHUNK, (c + 1) * RET_CHUNK) for c in range(RET_TILE // RET_CHUNK)]
    out_cols = [slice(n * OUT_COLS, (n + 1) * OUT_COLS) for n in range(D_MODEL // OUT_COLS)]
    gates = []
    for hd in range(RET_HEADS):
        lo = gate_off + hd * RET_DV
        gates.append(_dot(xs, w_in_ref[:, lo:lo + RET_DV]) * scale)
        recur(chunks[0], hd, gates[hd][chunks[0]])
    for prev, rows in zip(chunks[:-1], chunks[1:]):
        pieces = []
        for hd in range(RET_HEADS):
            recur(rows, hd, gates[hd][rows])
            if hd < len(out_cols):
                pieces.append(_dot(gated_ref[prev, :], w_out_ref[:, out_cols[hd]]))
        o_ref[prev, :] = x[prev] + jnp.concatenate(pieces, axis=-1)
    last = chunks[-1]
    o_ref[last, :] = x[last] + _dot(gated_ref[last, :], w_out_ref[...])


def _retention_layer(h, cos, sin, layer, j, norm_mix_g, w_in, gn_g, w_out):
    c, t = RET_CHUNK, RET_TILE
    log_g = jnp.log(1.0 - 2.0 ** (-5.0 - jnp.arange(RET_HEADS, dtype=F32)))
    idx = jnp.arange(c, dtype=F32)
    rel = idx[:, None] - idx[None, :]
    intra = jnp.where(rel[None] >= 0, jnp.exp(rel[None] * log_g[:, None, None]), 0.0)
    xi = jnp.exp((idx[None] + 1.0) * log_g[:, None])[..., None]
    zeta = jnp.exp((c - 1.0 - idx[None]) * log_g[:, None])[..., None]
    chunk_decay = jnp.exp(c * log_g)

    tile = pl.BlockSpec((None, t, D_MODEL), lambda b, s: (b, s, 0))
    trig = pl.BlockSpec((None, t, ROPE_HALF), lambda b, s: (b, s, 0))
    return pl.pallas_call(
        functools.partial(_retention_kernel, j=j),
        grid=(BATCH, SEQ // t),
        in_specs=[pl.BlockSpec(memory_space=pltpu.SMEM),
                  tile, trig, trig,
                  _layer_spec((1, D_MODEL), layer),
                  _HBM,
                  _const_spec((RET_HEADS, c, c)),
                  _const_spec((RET_HEADS, c, 1)),
                  _const_spec((RET_HEADS, c, 1)),
                  _layer_spec((1, 2 * D_MODEL), j),
                  _HBM],
        out_specs=tile,
        out_shape=jax.ShapeDtypeStruct(h.shape, F32),
        scratch_shapes=[pltpu.VMEM((RET_HEADS, RET_DK, RET_DV), F32),
                        pltpu.VMEM((t, 2 * D_MODEL), BF16)]
        + _weight_scratch((D_MODEL, RET_IN), (2 * D_MODEL, D_MODEL)) + _STAGE_SCRATCH,
        compiler_params=_params(*_SEQUENTIAL),
        name="retention_layer",
    )(chunk_decay, h, cos, sin, norm_mix_g.reshape(DEPTH, 1, D_MODEL), w_in, intra, xi, zeta,
      gn_g.reshape(-1, 1, 2 * D_MODEL), w_out)


def _gelu(x):
    return 0.5 * x * (1.0 + lax.erf(x * (2.0 ** -0.5)))


def _gmlp_kernel(h_ref, g_ref, w_in_hbm, ln_g_ref, ln_b_ref, w_s_ref, b_s_ref, w_out_hbm,
                 o_ref, v_ref, prod_ref, w_in_ref, w_out_ref, stage_ref, sem_ref, *, j):
    @pl.when(_first_step())
    def _():
        _stage_weights([(w_in_hbm.at[j], w_in_ref), (w_out_hbm.at[j], w_out_ref)],
                       stage_ref, sem_ref)

    gd = GMLP_GROUP_DIM
    groups = [slice(g * gd, (g + 1) * gd) for g in range(GMLP_GROUPS)]
    row = lax.broadcasted_iota(jnp.int32, (GMLP_CHUNK, GMLP_CHUNK), 0)
    col = lax.broadcasted_iota(jnp.int32, (GMLP_CHUNK, GMLP_CHUNK), 1)
    w_s = [jnp.where(row >= col, w_s_ref[g], 0.0).astype(BF16) for g in range(GMLP_GROUPS)]

    def gate_values(rows):
        x = h_ref[rows, :]
        xs, scale = _rms_split(x, g_ref[...])
        total = jnp.zeros((GMLP_SUB, 1), F32)
        for cols in groups:
            lo = GMLP_HALF + cols.start
            vg = _gelu(_dot(xs, w_in_ref[:, lo:lo + gd]) * scale)
            v_ref[rows, cols] = vg
            total = total + jnp.sum(vg, axis=-1, keepdims=True)
        mu = total * (1.0 / GMLP_HALF)
        sq = jnp.zeros((GMLP_SUB, 1), F32)
        for cols in groups:
            sq = sq + jnp.sum(jnp.square(v_ref[rows, cols] - mu), axis=-1, keepdims=True)
        return x, xs, scale, mu, lax.rsqrt(sq * (1.0 / GMLP_HALF) + GN_EPS)

    def mix(rows, xs, scale, mu, rstd):
        for g, cols in enumerate(groups):
            vn = ((v_ref[rows, cols] - mu) * rstd * ln_g_ref[:, cols] + ln_b_ref[:, cols]).astype(BF16)
            u = _gelu(_dot(xs, w_in_ref[:, cols]) * scale)
            for ch in range(GMLP_SUB // GMLP_CHUNK):
                sub = slice(ch * GMLP_CHUNK, (ch + 1) * GMLP_CHUNK)
                dst = slice(rows.start + sub.start, rows.start + sub.stop)
                sv = _dot(w_s[g], vn[sub]) + b_s_ref[g]
                prod_ref[dst, cols] = (u[sub] * sv).astype(BF16)

    def project(rows, x):
        o_ref[rows, :] = x + _dot(prod_ref[rows, :], w_out_ref[...])

    subs = [slice(i * GMLP_SUB, (i + 1) * GMLP_SUB) for i in range(GMLP_TILE // GMLP_SUB)]
    stats = [gate_values(rows) for rows in subs]
    for rows, (x, xs, scale, mu, rstd) in zip(subs, stats):
        mix(rows, xs, scale, mu, rstd)
        project(rows, x)


def _gmlp_layer(h, j, g_mix, w_in, ln_g, ln_b, w_s, b_s, w_out):
    tile = pl.BlockSpec((None, GMLP_TILE, D_MODEL), lambda b, s: (b, s, 0))
    return pl.pallas_call(
        functools.partial(_gmlp_kernel, j=j),
        grid=(BATCH, SEQ // GMLP_TILE),
        in_specs=[tile,
                  _const_spec((1, D_MODEL)),
                  _HBM,
                  _const_spec((1, GMLP_HALF)),
                  _const_spec((1, GMLP_HALF)),
                  _const_spec(w_s.shape),
                  _const_spec((GMLP_GROUPS, GMLP_CHUNK, 1)),
                  _HBM],
        out_specs=tile,
        out_shape=jax.ShapeDtypeStruct(h.shape, F32),
        scratch_shapes=[pltpu.VMEM((GMLP_TILE, GMLP_HALF), F32),
                        pltpu.VMEM((GMLP_TILE, GMLP_HALF), BF16)]
        + _weight_scratch((D_MODEL, 2 * GMLP_HALF), (GMLP_HALF, D_MODEL)) + _STAGE_SCRATCH,
        compiler_params=_params(*_SEQUENTIAL),
        name="gmlp_layer",
    )(h, g_mix.reshape(1, D_MODEL), w_in, ln_g.reshape(1, GMLP_HALF), ln_b.reshape(1, GMLP_HALF),
      w_s, b_s[:, :, None], w_out)


def _conv_kernel(h_ref, g_ref, w_in_hbm, k_ref, w_out_hbm, o_ref, y_ref, w_in_ref, w_out_ref,
                 stage_ref, sem_ref, *, j):
    @pl.when(_first_step())
    def _():
        _stage_weights([(w_in_hbm.at[j], w_in_ref), (w_out_hbm.at[j], w_out_ref)],
                       stage_ref, sem_ref)

    @pl.when(pl.program_id(1) == 0)
    def _():
        y_ref[0:SUBLANES, :] = jnp.zeros((SUBLANES, D_MODEL), F32)

    x = h_ref[...]
    xs, scale = _rms_split(x, g_ref[...])
    c_gate = _dot(xs, w_in_ref[:, D_MODEL:2 * D_MODEL]) * scale
    hid = _dot(xs, w_in_ref[:, 2 * D_MODEL:]) * scale
    y_ref[SUBLANES:, :] = c_gate * hid
    z = k_ref[CONV_WIDTH - 1:CONV_WIDTH, :] * y_ref[SUBLANES:, :]
    for lag in range(1, CONV_WIDTH):
        tap = k_ref[CONV_WIDTH - 1 - lag:CONV_WIDTH - lag, :]
        z = z + tap * y_ref[pl.ds(SUBLANES - lag, CONV_TILE), :]
    y_ref[0:SUBLANES, :] = y_ref[CONV_TILE:, :]
    b_gate = _dot(xs, w_in_ref[:, :D_MODEL]) * scale
    o_ref[...] = x + _dot((b_gate * z).astype(BF16), w_out_ref[...])


def _conv_layer(h, j, g_mix, w_in, conv_k, w_out):
    tile = pl.BlockSpec((None, CONV_TILE, D_MODEL), lambda b, s: (b, s, 0))
    return pl.pallas_call(
        functools.partial(_conv_kernel, j=j),
        grid=(BATCH, SEQ // CONV_TILE),
        in_specs=[tile,
                  _const_spec((1, D_MODEL)),
                  _HBM,
                  _const_spec((CONV_WIDTH, D_MODEL)),
                  _HBM],
        out_specs=tile,
        out_shape=jax.ShapeDtypeStruct(h.shape, F32),
        scratch_shapes=[pltpu.VMEM((SUBLANES + CONV_TILE, D_MODEL), F32)]
        + _weight_scratch((D_MODEL, 3 * D_MODEL), (D_MODEL, D_MODEL)) + _STAGE_SCRATCH,
        compiler_params=_params(*_SEQUENTIAL),
        name="conv_layer",
    )(h, g_mix.reshape(1, D_MODEL), w_in, conv_k, w_out)


def _mem_kv_kernel(mem_ref, g_ref, w_kv_hbm, kt_ref, v_ref, w_kv_ref, stage_ref, sem_ref):
    @pl.when(pl.program_id(1) == 0)
    def _():
        _stage_weights([(w_kv_hbm.at[pl.program_id(0)], w_kv_ref)], stage_ref, sem_ref)

    ms, scale = _rms_split(mem_ref[...], g_ref[...])
    kt_ref[...] = (_dot(ms, w_kv_ref[:, :D_MODEL]) * scale).T.astype(BF16)
    v_ref[...] = (_dot(ms, w_kv_ref[:, D_MODEL:]) * scale).astype(BF16)


def _mem_kv(mem, norm_mem_g, w_kv):
    return pl.pallas_call(
        _mem_kv_kernel,
        grid=(DEPTH, BATCH),
        in_specs=[pl.BlockSpec((None, MEM_LEN, D_MODEL), lambda i, b: (b, 0, 0)),
                  pl.BlockSpec((None, 1, D_MODEL), lambda i, b: (i, 0, 0)),
                  _HBM],
        out_specs=[pl.BlockSpec((None, None, D_MODEL, MEM_LEN), lambda i, b: (i, b, 0, 0)),
                   pl.BlockSpec((None, None, MEM_LEN, D_MODEL), lambda i, b: (i, b, 0, 0))],
        out_shape=[jax.ShapeDtypeStruct((DEPTH, BATCH, D_MODEL, MEM_LEN), BF16),
                   jax.ShapeDtypeStruct((DEPTH, BATCH, MEM_LEN, D_MODEL), BF16)],
        scratch_shapes=_weight_scratch((D_MODEL, 2 * D_MODEL)) + _STAGE_SCRATCH,
        compiler_params=_params(*_SEQUENTIAL),
        name="mem_kv",
    )(mem, norm_mem_g.reshape(DEPTH, 1, D_MODEL), w_kv)


def _xattn_ffn_kernel(h_ref, gx_ref, w_q_hbm, kt_ref, v_ref, w_o_hbm, gf_ref, w1_hbm, w2_hbm,
                      gfin_ref, o_ref, w_q_ref, w_o_ref, w1_ref, w2_ref, stage_ref, sem_ref, *,
                      layer, final_norm):
    @pl.when(_first_step())
    def _():
        _stage_weights([(w_q_hbm.at[layer], w_q_ref), (w_o_hbm.at[layer], w_o_ref),
                        (w1_hbm.at[layer], w1_ref), (w2_hbm.at[layer], w2_ref)],
                       stage_ref, sem_ref)

    head_cols = [slice(hd * XA_DH, (hd + 1) * XA_DH) for hd in range(XA_HEADS)]

    def query(rows):
        x = h_ref[rows, :]
        xs, scale = _rms_split(x, gx_ref[...])
        return x, (_dot(xs, w_q_ref[...]) * scale).astype(BF16)

    def probs(q):
        out = []
        for cols in head_cols:
            s = _dot(q[:, cols], kt_ref[cols, :]) * (XA_DH ** -0.5)
            e = jnp.exp(s - jnp.max(s, axis=-1, keepdims=True))
            out.append((e / jnp.sum(e, axis=-1, keepdims=True)).astype(BF16))
        return out

    def attend(x, p):
        heads = [_dot(ph, v_ref[:, cols]).astype(BF16) for ph, cols in zip(p, head_cols)]
        return x + _dot(jnp.concatenate(heads, axis=-1), w_o_ref[...])

    def hidden(y):
        ys, scale = _rms_split(y, gf_ref[...])
        acts = []
        for c in range(D_FF // FFN_COLS):
            cols = slice(c * FFN_COLS, (c + 1) * FFN_COLS)
            acts.append(jnp.square(jax.nn.relu(_dot(ys, w1_ref[:, cols]) * scale)).astype(BF16))
        return jnp.concatenate(acts, axis=-1)

    def finish(rows, y, act):
        out = y + _dot(act, w2_ref[...])
        o_ref[rows, :] = _rms(out, gfin_ref[...]) if final_norm else out

    rows_a, rows_b = (slice(i * XA_SUB, (i + 1) * XA_SUB) for i in range(XF_TILE // XA_SUB))
    x_a, q_a = query(rows_a)
    p_a = probs(q_a)
    x_b, q_b = query(rows_b)
    y_a = attend(x_a, p_a)
    p_b = probs(q_b)
    act_a = hidden(y_a)
    y_b = attend(x_b, p_b)
    finish(rows_a, y_a, act_a)
    finish(rows_b, y_b, hidden(y_b))


def _xattn_ffn_layer(h, layer, norm_xa_g, w_q, kt, v, w_o, norm_ffn_g, w1, w2, norm_f_g,
                     final_norm):
    tile = pl.BlockSpec((None, XF_TILE, D_MODEL), lambda b, j: (b, j, 0))
    return pl.pallas_call(
        functools.partial(_xattn_ffn_kernel, layer=layer, final_norm=final_norm),
        grid=(BATCH, SEQ // XF_TILE),
        in_specs=[tile,
                  _layer_spec((1, D_MODEL), layer),
                  _HBM,
                  pl.BlockSpec((None, None, D_MODEL, MEM_LEN), lambda b, j: (layer, b, 0, 0)),
                  pl.BlockSpec((None, None, MEM_LEN, D_MODEL), lambda b, j: (layer, b, 0, 0)),
                  _HBM,
                  _layer_spec((1, D_MODEL), layer),
                  _HBM,
                  _HBM,
                  _const_spec((1, D_MODEL))],
        out_specs=tile,
        out_shape=jax.ShapeDtypeStruct(h.shape, F32),
        scratch_shapes=_weight_scratch((D_MODEL, D_MODEL), (D_MODEL, D_MODEL), (D_MODEL, D_FF),
                                       (D_FF, D_MODEL)) + _STAGE_SCRATCH,
        compiler_params=_params(*_SEQUENTIAL),
        name="xattn_ffn_layer",
    )(h, norm_xa_g.reshape(DEPTH, 1, D_MODEL), w_q, kt, v, w_o,
      norm_ffn_g.reshape(DEPTH, 1, D_MODEL), w1, w2, norm_f_g.reshape(1, D_MODEL))


def kernel(x, mem, positions, norm_mix_g, norm_xa_g, norm_mem_g, xa_w_q, xa_w_kv, xa_w_o,
           norm_ffn_g, ffn_w1, ffn_w2, ret_w_in, ret_gn_g, ret_w_out,
           gmlp_w_in, gmlp_ln_g, gmlp_ln_b, gmlp_w_s, gmlp_b_s, gmlp_w_out,
           conv_w_in, conv_k, conv_w_out, norm_f_g):
    cos, sin = _rotary_tables(positions)
    kt, v = _mem_kv(mem, norm_mem_g, xa_w_kv)

    h = x
    for i in range(DEPTH):
        kind, j = i % N_MIXERS, i // N_MIXERS
        if kind == 0:
            h = _retention_layer(h, cos, sin, i, j, norm_mix_g, ret_w_in, ret_gn_g, ret_w_out)
        elif kind == 1:
            h = _gmlp_layer(h, j, norm_mix_g[i], gmlp_w_in, gmlp_ln_g[j], gmlp_ln_b[j],
                            gmlp_w_s[j], gmlp_b_s[j], gmlp_w_out)
        else:
            h = _conv_layer(h, j, norm_mix_g[i], conv_w_in, conv_k[j], conv_w_out)
        h = _xattn_ffn_layer(h, i, norm_xa_g, xa_w_q, kt, v, xa_w_o, norm_ffn_g, ffn_w1, ffn_w2,
                             norm_f_g, final_norm=(i == DEPTH - 1))
    return h
```

```python
import functools

import jax
import jax.numpy as jnp
from jax import lax
from jax.experimental import pallas as pl
from jax.experimental.pallas import tpu as pltpu

D_MODEL = 1024
BATCH = 8
SEQ = 2048
DEPTH = 4
N_MIXERS = 3

RET_HEADS = 4
RET_DK = D_MODEL // RET_HEADS
RET_DV = 2 * D_MODEL // RET_HEADS
RET_IN = 6 * D_MODEL
ROPE_BASE = 10000.0
ROPE_HALF = RET_DK // 2

GMLP_HALF = 3 * D_MODEL
GMLP_GROUPS = 4
GMLP_GROUP_DIM = GMLP_HALF // GMLP_GROUPS
GMLP_CHUNK = 128

CONV_WIDTH = 3

MEM_LEN = 256
XA_HEADS = 4
XA_DH = D_MODEL // XA_HEADS

D_FF = 4 * D_MODEL

NORM_EPS = 1e-6
GN_EPS = 1e-5

RET_CHUNK = 256
RET_TILE = 512
GMLP_TILE = 512
GMLP_SUB = 256
CONV_TILE = 512
XF_TILE = 1024
XA_SUB = 256
TRIG_TILE = 512
FFN_COLS = 1024
OUT_COLS = 512
SUBLANES = 8
STAGE_ROWS, STAGE_COLS = 512, 1024

VMEM_LIMIT_BYTES = 56 * 1024 * 1024

BF16 = jnp.bfloat16
F32 = jnp.float32


def _dot(a, b):
    return jnp.dot(a, b, preferred_element_type=F32)


def _dot_nt(a, b):
    return lax.dot_general(a, b, (((1,), (1,)), ((), ())), preferred_element_type=F32)


def _dot_tn(a, b):
    return lax.dot_general(a, b, (((0,), (0,)), ((), ())), preferred_element_type=F32)


def _rms(x, g):
    return x * lax.rsqrt(jnp.mean(x * x, axis=-1, keepdims=True) + NORM_EPS) * g


def _rms_split(x, g):
    scale = lax.rsqrt(jnp.mean(x * x, axis=-1, keepdims=True) + NORM_EPS)
    return (x * g).astype(BF16), scale


def _const_spec(shape):
    zeros = (0,) * len(shape)
    return pl.BlockSpec(shape, lambda *_: zeros, pipeline_mode=pl.Buffered(1))


def _layer_spec(shape, layer):
    idx = (layer,) + (0,) * len(shape)
    return pl.BlockSpec((None,) + shape, lambda *_: idx, pipeline_mode=pl.Buffered(1))


def _params(*semantics):
    return pltpu.CompilerParams(dimension_semantics=semantics,
                                vmem_limit_bytes=VMEM_LIMIT_BYTES)


_SEQUENTIAL = ("arbitrary", "arbitrary")
_HBM = pl.BlockSpec(memory_space=pl.ANY)
_STAGE_SCRATCH = [pltpu.VMEM((2, STAGE_ROWS, STAGE_COLS), F32), pltpu.SemaphoreType.DMA((2,))]


def _weight_scratch(*shapes):
    return [pltpu.VMEM(shape, BF16) for shape in shapes]


def _first_step():
    return jnp.logical_and(pl.program_id(0) == 0, pl.program_id(1) == 0)


def _stage_weights(jobs, stage_ref, sem_ref):
    blocks = []
    for src, dst in jobs:
        rows, cols = dst.shape
        assert rows % STAGE_ROWS == 0 and cols % STAGE_COLS == 0, dst.shape
        blocks += [(src, dst, r, c) for r in range(0, rows, STAGE_ROWS)
                   for c in range(0, cols, STAGE_COLS)]

    def fetch(n):
        src, _, r, c = blocks[n]
        return pltpu.make_async_copy(src.at[pl.ds(r, STAGE_ROWS), pl.ds(c, STAGE_COLS)],
                                     stage_ref.at[n % 2], sem_ref.at[n % 2])

    fetch(0).start()
    for n, (_, dst, r, c) in enumerate(blocks):
        if n + 1 < len(blocks):
            fetch(n + 1).start()
        fetch(n).wait()
        dst[r:r + STAGE_ROWS, c:c + STAGE_COLS] = stage_ref[n % 2].astype(BF16)


def _retention_kernel(decay_ref, h_ref, cos_ref, sin_ref, g_ref, w_in_hbm, intra_ref, xi_ref,
                      zeta_ref, gn_ref, w_out_hbm, o_ref, state_ref, gated_ref, w_in_ref,
                      w_out_ref, stage_ref, sem_ref, *, j):
    @pl.when(_first_step())
    def _():
        _stage_weights([(w_in_hbm.at[j], w_in_ref), (w_out_hbm.at[j], w_out_ref)],
                       stage_ref, sem_ref)

    @pl.when(pl.program_id(1) == 0)
    def _():
        state_ref[...] = jnp.zeros_like(state_ref)

    x = h_ref[...]
    xs, scale = _rms_split(x, g_ref[...])

    k_off, v_off, gate_off = D_MODEL, 2 * D_MODEL, 4 * D_MODEL
    q_all = _dot(xs, w_in_ref[:, :k_off]) * scale
    k_all = _dot(xs, w_in_ref[:, k_off:v_off]) * (scale * RET_DK ** -0.5)
    v_all = (_dot(xs, w_in_ref[:, v_off:gate_off]) * scale).astype(BF16)

    def recur(rows, hd, gate):
        cos = cos_ref[rows, :]
        sin = sin_ref[rows, :]

        def rotate(t):
            t1, t2 = t[:, :ROPE_HALF], t[:, ROPE_HALF:]
            return jnp.concatenate([t1 * cos - t2 * sin, t2 * cos + t1 * sin], axis=-1)

        dk = slice(hd * RET_DK, (hd + 1) * RET_DK)
        dv = slice(hd * RET_DV, (hd + 1) * RET_DV)
        q = rotate(q_all[rows, dk]).astype(BF16)
        k = rotate(k_all[rows, dk])
        v = v_all[rows, dv]
        state = state_ref[hd]
        s = _dot_nt(q, k.astype(BF16)) * intra_ref[hd]
        o = _dot(s.astype(BF16), v) + _dot(q, state.astype(BF16)) * xi_ref[hd]
        state_ref[hd] = state * decay_ref[hd] + _dot_tn((k * zeta_ref[hd]).astype(BF16), v)
        mu = jnp.mean(o, axis=-1, keepdims=True)
        var = jnp.mean(jnp.square(o - mu), axis=-1, keepdims=True)
        on = (o - mu) * lax.rsqrt(var + GN_EPS) * gn_ref[:, dv]
        gated_ref[rows, dv] = (jax.nn.silu(gate) * on).astype(BF16)

    chunks = [slice(c * RET_CHUNK, (c + 1) * RET_CHUNK) for c in range(RET_TILE // RET_CHUNK)]
    out_cols = [slice(n * OUT_COLS, (n + 1) * OUT_COLS) for n in range(D_MODEL // OUT_COLS)]
    gates = []
    for hd in range(RET_HEADS):
        lo = gate_off + hd * RET_DV
        gates.append(_dot(xs, w_in_ref[:, lo:lo + RET_DV]) * scale)
        recur(chunks[0], hd, gates[hd][chunks[0]])
    for prev, rows in zip(chunks[:-1], chunks[1:]):
        pieces = []
        for hd in range(RET_HEADS):
            recur(rows, hd, gates[hd][rows])
            if hd < len(out_cols):
                pieces.append(_dot(gated_ref[prev, :], w_out_ref[:, out_cols[hd]]))
        o_ref[prev, :] = x[prev] + jnp.concatenate(pieces, axis=-1)
    last = chunks[-1]
    o_ref[last, :] = x[last] + _dot(gated_ref[last, :], w_out_ref[...])


def _retention_layer(h, cos, sin, layer, j, norm_mix_g, w_in, gn_g, w_out):
    c, t = RET_CHUNK, RET_TILE
    log_g = jnp.log(1.0 - 2.0 ** (-5.0 - jnp.arange(RET_HEADS, dtype=F32)))
    idx = jnp.arange(c, dtype=F32)
    rel = idx[:, None] - idx[None, :]
    intra = jnp.where(rel[None] >= 0, jnp.exp(rel[None] * log_g[:, None, None]), 0.0)
    xi = jnp.exp((idx[None] + 1.0) * log_g[:, None])[..., None]
    zeta = jnp.exp((c - 1.0 - idx[None]) * log_g[:, None])[..., None]
    chunk_decay = jnp.exp(c * log_g)

    tile = pl.BlockSpec((None, t, D_MODEL), lambda b, s: (b, s, 0))
    trig = pl.BlockSpec((None, t, ROPE_HALF), lambda b, s: (b, s, 0))
    return pl.pallas_call(
        functools.partial(_retention_kernel, j=j),
        grid=(BATCH, SEQ // t),
        in_specs=[pl.BlockSpec(memory_space=pltpu.SMEM),
                  tile, trig, trig,
                  _layer_spec((1, D_MODEL), layer),
                  _HBM,
                  _const_spec((RET_HEADS, c, c)),
                  _const_spec((RET_HEADS, c, 1)),
                  _const_spec((RET_HEADS, c, 1)),
                  _layer_spec((1, 2 * D_MODEL), j),
                  _HBM],
        out_specs=tile,
        out_shape=jax.ShapeDtypeStruct(h.shape, F32),
        scratch_shapes=[pltpu.VMEM((RET_HEADS, RET_DK, RET_DV), F32),
                        pltpu.VMEM((t, 2 * D_MODEL), BF16)]
        + _weight_scratch((D_MODEL, RET_IN), (2 * D_MODEL, D_MODEL)) + _STAGE_SCRATCH,
        compiler_params=_params(*_SEQUENTIAL),
        name="retention_layer",
    )(chunk_decay, h, cos, sin, norm_mix_g.reshape(DEPTH, 1, D_MODEL), w_in, intra, xi, zeta,
      gn_g.reshape(-1, 1, 2 * D_MODEL), w_out)


def _gelu(x):
    return 0.5 * x * (1.0 + lax.erf(x * (2.0 ** -0.5)))


def _gmlp_kernel(h_ref, g_ref, w_in_hbm, ln_g_ref, ln_b_ref, w_s_ref, b_s_ref, w_out_hbm,
                 o_ref, v_ref, prod_ref, w_in_ref, w_out_ref, stage_ref, sem_ref, *, j):
    @pl.when(_first_step())
    def _():
        _stage_weights([(w_in_hbm.at[j], w_in_ref), (w_out_hbm.at[j], w_out_ref)],
                       stage_ref, sem_ref)

    gd = GMLP_GROUP_DIM
    groups = [slice(g * gd, (g + 1) * gd) for g in range(GMLP_GROUPS)]
    row = lax.broadcasted_iota(jnp.int32, (GMLP_CHUNK, GMLP_CHUNK), 0)
    col = lax.broadcasted_iota(jnp.int32, (GMLP_CHUNK, GMLP_CHUNK), 1)
    w_s = [jnp.where(row >= col, w_s_ref[g], 0.0).astype(BF16) for g in range(GMLP_GROUPS)]

    def gate_values(rows):
        x = h_ref[rows, :]
        xs, scale = _rms_split(x, g_ref[...])
        total = jnp.zeros((GMLP_SUB, 1), F32)
        for cols in groups:
            lo = GMLP_HALF + cols.start
            vg = _gelu(_dot(xs, w_in_ref[:, lo:lo + gd]) * scale)
            v_ref[rows, cols] = vg
            total = total + jnp.sum(vg, axis=-1, keepdims=True)
        mu = total * (1.0 / GMLP_HALF)
        sq = jnp.zeros((GMLP_SUB, 1), F32)
        for cols in groups:
            sq = sq + jnp.sum(jnp.square(v_ref[rows, cols] - mu), axis=-1, keepdims=True)
        return x, xs, scale, mu, lax.rsqrt(sq * (1.0 / GMLP_HALF) + GN_EPS)

    def mix(rows, xs, scale, mu, rstd):
        for g, cols in enumerate(groups):
            vn = ((v_ref[rows, cols] - mu) * rstd * ln_g_ref[:, cols] + ln_b_ref[:, cols]).astype(BF16)
            u = _gelu(_dot(xs, w_in_ref[:, cols]) * scale)
            for ch in range(GMLP_SUB // GMLP_CHUNK):
                sub = slice(ch * GMLP_CHUNK, (ch + 1) * GMLP_CHUNK)
                dst = slice(rows.start + sub.start, rows.start + sub.stop)
                sv = _dot(w_s[g], vn[sub]) + b_s_ref[g]
                prod_ref[dst, cols] = (u[sub] * sv).astype(BF16)

    def project(rows, x):
        o_ref[rows, :] = x + _dot(prod_ref[rows, :], w_out_ref[...])

    subs = [slice(i * GMLP_SUB, (i + 1) * GMLP_SUB) for i in range(GMLP_TILE // GMLP_SUB)]
    stats = [gate_values(rows) for rows in subs]
    for rows, (x, xs, scale, mu, rstd) in zip(subs, stats):
        mix(rows, xs, scale, mu, rstd)
        project(rows, x)


def _gmlp_layer(h, j, g_mix, w_in, ln_g, ln_b, w_s, b_s, w_out):
    tile = pl.BlockSpec((None, GMLP_TILE, D_MODEL), lambda b, s: (b, s, 0))
    return pl.pallas_call(
        functools.partial(_gmlp_kernel, j=j),
        grid=(BATCH, SEQ // GMLP_TILE),
        in_specs=[tile,
                  _const_spec((1, D_MODEL)),
                  _HBM,
                  _const_spec((1, GMLP_HALF)),
                  _const_spec((1, GMLP_HALF)),
                  _const_spec(w_s.shape),
                  _const_spec((GMLP_GROUPS, GMLP_CHUNK, 1)),
                  _HBM],
        out_specs=tile,
        out_shape=jax.ShapeDtypeStruct(h.shape, F32),
        scratch_shapes=[pltpu.VMEM((GMLP_TILE, GMLP_HALF), F32),
                        pltpu.VMEM((GMLP_TILE, GMLP_HALF), BF16)]
        + _weight_scratch((D_MODEL, 2 * GMLP_HALF), (GMLP_HALF, D_MODEL)) + _STAGE_SCRATCH,
        compiler_params=_params(*_SEQUENTIAL),
        name="gmlp_layer",
    )(h, g_mix.reshape(1, D_MODEL), w_in, ln_g.reshape(1, GMLP_HALF), ln_b.reshape(1, GMLP_HALF),
      w_s, b_s[:, :, None], w_out)


def _conv_kernel(h_ref, g_ref, w_in_hbm, k_ref, w_out_hbm, o_ref, y_ref, w_in_ref, w_out_ref,
                 stage_ref, sem_ref, *, j):
    @pl.when(_first_step())
    def _():
        _stage_weights([(w_in_hbm.at[j], w_in_ref), (w_out_hbm.at[j], w_out_ref)],
                       stage_ref, sem_ref)

    @pl.when(pl.program_id(1) == 0)
    def _():
        y_ref[0:SUBLANES, :] = jnp.zeros((SUBLANES, D_MODEL), F32)

    x = h_ref[...]
    xs, scale = _rms_split(x, g_ref[...])
    c_gate = _dot(xs, w_in_ref[:, D_MODEL:2 * D_MODEL]) * scale
    hid = _dot(xs, w_in_ref[:, 2 * D_MODEL:]) * scale
    y_ref[SUBLANES:, :] = c_gate * hid
    z = k_ref[CONV_WIDTH - 1:CONV_WIDTH, :] * y_ref[SUBLANES:, :]
    for lag in range(1, CONV_WIDTH):
        tap = k_ref[CONV_WIDTH - 1 - lag:CONV_WIDTH - lag, :]
        z = z + tap * y_ref[pl.ds(SUBLANES - lag, CONV_TILE), :]
    y_ref[0:SUBLANES, :] = y_ref[CONV_TILE:, :]
    b_gate = _dot(xs, w_in_ref[:, :D_MODEL]) * scale
    o_ref[...] = x + _dot((b_gate * z).astype(BF16), w_out_ref[...])


def _conv_layer(h, j, g_mix, w_in, conv_k, w_out):
    tile = pl.BlockSpec((None, CONV_TILE, D_MODEL), lambda b, s: (b, s, 0))
    return pl.pallas_call(
        functools.partial(_conv_kernel, j=j),
        grid=(BATCH, SEQ // CONV_TILE),
        in_specs=[tile,
                  _const_spec((1, D_MODEL)),
                  _HBM,
                  _const_spec((CONV_WIDTH, D_MODEL)),
                  _HBM],
        out_specs=tile,
        out_shape=jax.ShapeDtypeStruct(h.shape, F32),
        scratch_shapes=[pltpu.VMEM((SUBLANES + CONV_TILE, D_MODEL), F32)]
        + _weight_scratch((D_MODEL, 3 * D_MODEL), (D_MODEL, D_MODEL)) + _STAGE_SCRATCH,
        compiler_params=_params(*_SEQUENTIAL),
        name="conv_layer",
    )(h, g_mix.reshape(1, D_MODEL), w_in, conv_k, w_out)


def _prelude_kernel(mem_ref, g_ref, w_kv_hbm, pos_ref, inv_ref, kt_ref, v_ref, cos_ref, sin_ref,
                    w_kv_ref, stage_ref, sem_ref):
    @pl.when(pl.program_id(1) == 0)
    def _():
        _stage_weights([(w_kv_hbm.at[pl.program_id(0)], w_kv_ref)], stage_ref, sem_ref)

    ms, scale = _rms_split(mem_ref[...], g_ref[...])
    k = _dot(ms, w_kv_ref[:, :D_MODEL])
    v = _dot(ms, w_kv_ref[:, D_MODEL:])
    ang = pos_ref[...].astype(F32) * inv_ref[...]
    cos_ref[...] = jnp.cos(ang)
    sin_ref[...] = jnp.sin(ang)
    kt_ref[...] = (k * scale).T.astype(BF16)
    v_ref[...] = (v * scale).astype(BF16)


def _prelude(mem, norm_mem_g, w_kv, positions):
    tiles_per_row = SEQ // TRIG_TILE
    assert DEPTH * BATCH == BATCH * tiles_per_row

    def trig_index(i, b):
        step = i * BATCH + b
        return (step // tiles_per_row, step % tiles_per_row, 0)

    inv_freq = ROPE_BASE ** (-jnp.arange(ROPE_HALF, dtype=F32) / ROPE_HALF)
    trig = jax.ShapeDtypeStruct((BATCH, SEQ, ROPE_HALF), F32)
    trig_tile = pl.BlockSpec((None, TRIG_TILE, ROPE_HALF), trig_index)
    return pl.pallas_call(
        _prelude_kernel,
        grid=(DEPTH, BATCH),
        in_specs=[pl.BlockSpec((None, MEM_LEN, D_MODEL), lambda i, b: (b, 0, 0)),
                  pl.BlockSpec((None, 1, D_MODEL), lambda i, b: (i, 0, 0)),
                  _HBM,
                  pl.BlockSpec((None, TRIG_TILE, 1), trig_index),
                  _const_spec((1, ROPE_HALF))],
        out_specs=[pl.BlockSpec((None, None, D_MODEL, MEM_LEN), lambda i, b: (i, b, 0, 0)),
                   pl.BlockSpec((None, None, MEM_LEN, D_MODEL), lambda i, b: (i, b, 0, 0)),
                   trig_tile, trig_tile],
        out_shape=[jax.ShapeDtypeStruct((DEPTH, BATCH, D_MODEL, MEM_LEN), BF16),
                   jax.ShapeDtypeStruct((DEPTH, BATCH, MEM_LEN, D_MODEL), BF16),
                   trig, trig],
        scratch_shapes=_weight_scratch((D_MODEL, 2 * D_MODEL)) + _STAGE_SCRATCH,
        compiler_params=_params(*_SEQUENTIAL),
        name="prelude",
    )(mem, norm_mem_g.reshape(DEPTH, 1, D_MODEL), w_kv, positions.reshape(BATCH, SEQ, 1),
      inv_freq.reshape(1, ROPE_HALF))


def _xattn_ffn_kernel(h_ref, gx_ref, w_q_hbm, kt_ref, v_ref, w_o_hbm, gf_ref, w1_hbm, w2_hbm,
                      gfin_ref, o_ref, w_q_ref, w_o_ref, w1_ref, w2_ref, stage_ref, sem_ref, *,
                      layer, final_norm):
    @pl.when(_first_step())
    def _():
        _stage_weights([(w_q_hbm.at[layer], w_q_ref), (w_o_hbm.at[layer], w_o_ref),
                        (w1_hbm.at[layer], w1_ref), (w2_hbm.at[layer], w2_ref)],
                       stage_ref, sem_ref)

    head_cols = [slice(hd * XA_DH, (hd + 1) * XA_DH) for hd in range(XA_HEADS)]

    def query(rows):
        x = h_ref[rows, :]
        xs, scale = _rms_split(x, gx_ref[...])
        return x, (_dot(xs, w_q_ref[...]) * scale).astype(BF16)

    def probs(q):
        out = []
        for cols in head_cols:
            s = _dot(q[:, cols], kt_ref[cols, :]) * (XA_DH ** -0.5)
            e = jnp.exp(s - jnp.max(s, axis=-1, keepdims=True))
            out.append((e / jnp.sum(e, axis=-1, keepdims=True)).astype(BF16))
        return out

    def attend(x, p):
        heads = [_dot(ph, v_ref[:, cols]).astype(BF16) for ph, cols in zip(p, head_cols)]
        return x + _dot(jnp.concatenate(heads, axis=-1), w_o_ref[...])

    def hidden(y):
        ys, scale = _rms_split(y, gf_ref[...])
        acts = []
        for c in range(D_FF // FFN_COLS):
            cols = slice(c * FFN_COLS, (c + 1) * FFN_COLS)
            acts.append(jnp.square(jax.nn.relu(_dot(ys, w1_ref[:, cols]) * scale)).astype(BF16))
        return jnp.concatenate(acts, axis=-1)

    def finish(rows, y, act):
        out = y + _dot(act, w2_ref[...])
        o_ref[rows, :] = _rms(out, gfin_ref[...]) if final_norm else out

    subs = [slice(i * XA_SUB, (i + 1) * XA_SUB) for i in range(XF_TILE // XA_SUB)]
    x, q = query(subs[0])
    p = probs(q)
    for i, rows in enumerate(subs):
        ahead = query(subs[i + 1]) if i + 1 < len(subs) else None
        y = attend(x, p)
        if ahead is not None:
            x, p = ahead[0], probs(ahead[1])
        finish(rows, y, hidden(y))


def _xattn_ffn_layer(h, layer, norm_xa_g, w_q, kt, v, w_o, norm_ffn_g, w1, w2, norm_f_g,
                     final_norm):
    tile = pl.BlockSpec((None, XF_TILE, D_MODEL), lambda b, j: (b, j, 0))
    return pl.pallas_call(
        functools.partial(_xattn_ffn_kernel, layer=layer, final_norm=final_norm),
        grid=(BATCH, SEQ // XF_TILE),
        in_specs=[tile,
                  _layer_spec((1, D_MODEL), layer),
                  _HBM,
                  pl.BlockSpec((None, None, D_MODEL, MEM_LEN), lambda b, j: (layer, b, 0, 0)),
                  pl.BlockSpec((None, None, MEM_LEN, D_MODEL), lambda b, j: (layer, b, 0, 0)),
                  _HBM,
                  _layer_spec((1, D_MODEL), layer),
                  _HBM,
                  _HBM,
                  _const_spec((1, D_MODEL))],
        out_specs=tile,
        out_shape=jax.ShapeDtypeStruct(h.shape, F32),
        scratch_shapes=_weight_scratch((D_MODEL, D_MODEL), (D_MODEL, D_MODEL), (D_MODEL, D_FF),
                                       (D_FF, D_MODEL)) + _STAGE_SCRATCH,
        compiler_params=_params(*_SEQUENTIAL),
        name="xattn_ffn_layer",
    )(h, norm_xa_g.reshape(DEPTH, 1, D_MODEL), w_q, kt, v, w_o,
      norm_ffn_g.reshape(DEPTH, 1, D_MODEL), w1, w2, norm_f_g.reshape(1, D_MODEL))


def kernel(x, mem, positions, norm_mix_g, norm_xa_g, norm_mem_g, xa_w_q, xa_w_kv, xa_w_o,
           norm_ffn_g, ffn_w1, ffn_w2, ret_w_in, ret_gn_g, ret_w_out,
           gmlp_w_in, gmlp_ln_g, gmlp_ln_b, gmlp_w_s, gmlp_b_s, gmlp_w_out,
           conv_w_in, conv_k, conv_w_out, norm_f_g):
    kt, v, cos, sin = _prelude(mem, norm_mem_g, xa_w_kv, positions)

    h = x
    for i in range(DEPTH):
        kind, j = i % N_MIXERS, i // N_MIXERS
        if kind == 0:
            h = _retention_layer(h, cos, sin, i, j, norm_mix_g, ret_w_in, ret_gn_g, ret_w_out)
        elif kind == 1:
            h = _gmlp_layer(h, j, norm_mix_g[i], gmlp_w_in, gmlp_ln_g[j], gmlp_ln_b[j],
                            gmlp_w_s[j], gmlp_b_s[j], gmlp_w_out)
        else:
            h = _conv_layer(h, j, norm_mix_g[i], conv_w_in, conv_k[j], conv_w_out)
        h = _xattn_ffn_layer(h, i, norm_xa_g, xa_w_q, kt, v, xa_w_o, norm_ffn_g, ffn_w1, ffn_w2,
                             norm_f_g, final_norm=(i == DEPTH - 1))
    return h
```

```python
import functools

import jax
import jax.numpy as jnp
from jax import lax
from jax.experimental import pallas as pl
from jax.experimental.pallas import tpu as pltpu

D_MODEL = 1024
BATCH = 8
SEQ = 2048
DEPTH = 4
N_MIXERS = 3

RET_HEADS = 4
RET_DK = D_MODEL // RET_HEADS
RET_DV = 2 * D_MODEL // RET_HEADS
RET_IN = 6 * D_MODEL
ROPE_BASE = 10000.0
ROPE_HALF = RET_DK // 2

GMLP_HALF = 3 * D_MODEL
GMLP_GROUPS = 4
GMLP_GROUP_DIM = GMLP_HALF // GMLP_GROUPS
GMLP_CHUNK = 128

CONV_WIDTH = 3

MEM_LEN = 256
XA_HEADS = 4
XA_DH = D_MODEL // XA_HEADS

D_FF = 4 * D_MODEL

NORM_EPS = 1e-6
GN_EPS = 1e-5

RET_CHUNK = 256
RET_TILE = 512
GMLP_TILE = 512
GMLP_SUB = 256
CONV_TILE = 512
XF_TILE = 512
XA_SUB = 256
TRIG_TILE = 512
FFN_COLS = 1024
OUT_COLS = 512
SUBLANES = 8
STAGE_ROWS, STAGE_COLS = 512, 1024

VMEM_LIMIT_BYTES = 56 * 1024 * 1024

BF16 = jnp.bfloat16
F32 = jnp.float32


def _dot(a, b):
    return jnp.dot(a, b, preferred_element_type=F32)


def _dot_nt(a, b):
    return lax.dot_general(a, b, (((1,), (1,)), ((), ())), preferred_element_type=F32)


def _dot_tn(a, b):
    return lax.dot_general(a, b, (((0,), (0,)), ((), ())), preferred_element_type=F32)


def _rms(x, g):
    return x * lax.rsqrt(jnp.mean(x * x, axis=-1, keepdims=True) + NORM_EPS) * g


def _rms_split(x, g):
    operand = (x * g).astype(BF16)
    scale = lax.rsqrt(jnp.mean(x * x, axis=-1, keepdims=True) + NORM_EPS)
    return operand, scale


def _const_spec(shape):
    zeros = (0,) * len(shape)
    return pl.BlockSpec(shape, lambda *_: zeros, pipeline_mode=pl.Buffered(1))


def _layer_spec(shape, layer):
    idx = (layer,) + (0,) * len(shape)
    return pl.BlockSpec((None,) + shape, lambda *_: idx, pipeline_mode=pl.Buffered(1))


def _params(*semantics):
    return pltpu.CompilerParams(dimension_semantics=semantics,
                                vmem_limit_bytes=VMEM_LIMIT_BYTES)


_SEQUENTIAL = ("arbitrary", "arbitrary")
_HBM = pl.BlockSpec(memory_space=pl.ANY)
_STAGE_SCRATCH = [pltpu.VMEM((2, STAGE_ROWS, STAGE_COLS), F32), pltpu.SemaphoreType.DMA((2,))]


def _weight_scratch(*shapes):
    return [pltpu.VMEM(shape, BF16) for shape in shapes]


def _first_step():
    return jnp.logical_and(pl.program_id(0) == 0, pl.program_id(1) == 0)


def _stage_weights(jobs, stage_ref, sem_ref):
    blocks = []
    for src, dst in jobs:
        rows, cols = dst.shape
        assert rows % STAGE_ROWS == 0 and cols % STAGE_COLS == 0, dst.shape
        blocks += [(src, dst, r, c) for r in range(0, rows, STAGE_ROWS)
                   for c in range(0, cols, STAGE_COLS)]

    def fetch(n):
        src, _, r, c = blocks[n]
        return pltpu.make_async_copy(src.at[pl.ds(r, STAGE_ROWS), pl.ds(c, STAGE_COLS)],
                                     stage_ref.at[n % 2], sem_ref.at[n % 2])

    fetch(0).start()
    for n, (_, dst, r, c) in enumerate(blocks):
        if n + 1 < len(blocks):
            fetch(n + 1).start()
        fetch(n).wait()
        dst[r:r + STAGE_ROWS, c:c + STAGE_COLS] = stage_ref[n % 2].astype(BF16)


def _retention_kernel(decay_ref, h_ref, trig_a_ref, trig_b_ref, g_ref, w_in_hbm, intra_ref,
                      xi_ref, zeta_ref, gn_ref, w_out_hbm, o_ref, *rest, j, make_tables):
    if make_tables:
        cos_out_ref, sin_out_ref, *rest = rest
    state_ref, gated_ref, w_in_ref, w_out_ref, stage_ref, sem_ref = rest

    @pl.when(_first_step())
    def _():
        _stage_weights([(w_in_hbm.at[j], w_in_ref), (w_out_hbm.at[j], w_out_ref)],
                       stage_ref, sem_ref)

    @pl.when(pl.program_id(1) == 0)
    def _():
        state_ref[...] = jnp.zeros_like(state_ref)

    x = h_ref[...]
    xs, scale = _rms_split(x, g_ref[...])
    if make_tables:
        ang = trig_a_ref[...].astype(F32) * trig_b_ref[...]
        cos_all, sin_all = jnp.cos(ang), jnp.sin(ang)
        cos_out_ref[...] = cos_all
        sin_out_ref[...] = sin_all
    else:
        cos_all, sin_all = trig_a_ref[...], trig_b_ref[...]

    k_off, v_off, gate_off = D_MODEL, 2 * D_MODEL, 4 * D_MODEL
    q_all = _dot(xs, w_in_ref[:, :k_off]) * scale
    k_all = _dot(xs, w_in_ref[:, k_off:v_off]) * (scale * RET_DK ** -0.5)
    v_all = (_dot(xs, w_in_ref[:, v_off:gate_off]) * scale).astype(BF16)

    def recur(rows, hd, gate):
        cos = cos_all[rows]
        sin = sin_all[rows]

        def rotate(t):
            t1, t2 = t[:, :ROPE_HALF], t[:, ROPE_HALF:]
            return jnp.concatenate([t1 * cos - t2 * sin, t2 * cos + t1 * sin], axis=-1)

        dk = slice(hd * RET_DK, (hd + 1) * RET_DK)
        dv = slice(hd * RET_DV, (hd + 1) * RET_DV)
        q = rotate(q_all[rows, dk]).astype(BF16)
        k = rotate(k_all[rows, dk])
        v = v_all[rows, dv]
        state = state_ref[hd]
        s = _dot_nt(q, k.astype(BF16)) * intra_ref[hd]
        o = _dot(s.astype(BF16), v) + _dot(q, state.astype(BF16)) * xi_ref[hd]
        state_ref[hd] = state * decay_ref[hd] + _dot_tn((k * zeta_ref[hd]).astype(BF16), v)
        mu = jnp.mean(o, axis=-1, keepdims=True)
        var = jnp.mean(jnp.square(o - mu), axis=-1, keepdims=True)
        on = (o - mu) * lax.rsqrt(var + GN_EPS) * gn_ref[:, dv]
        gated_ref[rows, dv] = (jax.nn.silu(gate) * on).astype(BF16)

    chunks = [slice(c * RET_CHUNK, (c + 1) * RET_CHUNK) for c in range(RET_TILE // RET_CHUNK)]
    out_cols = [slice(n * OUT_COLS, (n + 1) * OUT_COLS) for n in range(D_MODEL // OUT_COLS)]
    gates = []
    for hd in range(RET_HEADS):
        lo = gate_off + hd * RET_DV
        gates.append(_dot(xs, w_in_ref[:, lo:lo + RET_DV]) * scale)
        recur(chunks[0], hd, gates[hd][chunks[0]])
    for prev, rows in zip(chunks[:-1], chunks[1:]):
        pieces = []
        for hd in range(RET_HEADS):
            recur(rows, hd, gates[hd][rows])
            if hd < len(out_cols):
                pieces.append(_dot(gated_ref[prev, :], w_out_ref[:, out_cols[hd]]))
        o_ref[prev, :] = x[prev] + jnp.concatenate(pieces, axis=-1)
    last = chunks[-1]
    o_ref[last, :] = x[last] + _dot(gated_ref[last, :], w_out_ref[...])


def _retention_layer(h, tables, positions, layer, j, norm_mix_g, w_in, gn_g, w_out):
    c, t = RET_CHUNK, RET_TILE
    log_g = jnp.log(1.0 - 2.0 ** (-5.0 - jnp.arange(RET_HEADS, dtype=F32)))
    idx = jnp.arange(c, dtype=F32)
    rel = idx[:, None] - idx[None, :]
    intra = jnp.where(rel[None] >= 0, jnp.exp(rel[None] * log_g[:, None, None]), 0.0)
    xi = jnp.exp((idx[None] + 1.0) * log_g[:, None])[..., None]
    zeta = jnp.exp((c - 1.0 - idx[None]) * log_g[:, None])[..., None]
    chunk_decay = jnp.exp(c * log_g)

    tile = pl.BlockSpec((None, t, D_MODEL), lambda b, s: (b, s, 0))
    trig = pl.BlockSpec((None, t, ROPE_HALF), lambda b, s: (b, s, 0))
    make_tables = tables is None
    if make_tables:
        inv_freq = ROPE_BASE ** (-jnp.arange(ROPE_HALF, dtype=F32) / ROPE_HALF)
        trig_args = (positions.reshape(BATCH, SEQ, 1), inv_freq.reshape(1, ROPE_HALF))
        trig_specs = [pl.BlockSpec((None, t, 1), lambda b, s: (b, s, 0)),
                      _const_spec((1, ROPE_HALF))]
        table_shape = jax.ShapeDtypeStruct((BATCH, SEQ, ROPE_HALF), F32)
        out_specs = [tile, trig, trig]
        out_shape = [jax.ShapeDtypeStruct(h.shape, F32), table_shape, table_shape]
    else:
        trig_args, trig_specs = tuple(tables), [trig, trig]
        out_specs, out_shape = [tile], [jax.ShapeDtypeStruct(h.shape, F32)]
    out = pl.pallas_call(
        functools.partial(_retention_kernel, j=j, make_tables=make_tables),
        grid=(BATCH, SEQ // t),
        in_specs=[pl.BlockSpec(memory_space=pltpu.SMEM),
                  tile, *trig_specs,
                  _layer_spec((1, D_MODEL), layer),
                  _HBM,
                  _const_spec((RET_HEADS, c, c)),
                  _const_spec((RET_HEADS, c, 1)),
                  _const_spec((RET_HEADS, c, 1)),
                  _layer_spec((1, 2 * D_MODEL), j),
                  _HBM],
        out_specs=out_specs,
        out_shape=out_shape,
        scratch_shapes=[pltpu.VMEM((RET_HEADS, RET_DK, RET_DV), F32),
                        pltpu.VMEM((t, 2 * D_MODEL), BF16)]
        + _weight_scratch((D_MODEL, RET_IN), (2 * D_MODEL, D_MODEL)) + _STAGE_SCRATCH,
        compiler_params=_params(*_SEQUENTIAL),
        name="retention_layer",
    )(chunk_decay, h, *trig_args, norm_mix_g.reshape(DEPTH, 1, D_MODEL), w_in, intra, xi, zeta,
      gn_g.reshape(-1, 1, 2 * D_MODEL), w_out)
    return out[0], (tuple(out[1:]) if make_tables else tuple(tables))


def _gelu(x):
    return 0.5 * x * (1.0 + lax.erf(x * (2.0 ** -0.5)))


def _gmlp_kernel(h_ref, g_ref, w_in_hbm, ln_g_ref, ln_b_ref, w_s_ref, b_s_ref, w_out_hbm,
                 o_ref, v_ref, prod_ref, w_in_ref, w_out_ref, stage_ref, sem_ref, *, j):
    @pl.when(_first_step())
    def _():
        _stage_weights([(w_in_hbm.at[j], w_in_ref), (w_out_hbm.at[j], w_out_ref)],
                       stage_ref, sem_ref)

    gd = GMLP_GROUP_DIM
    groups = [slice(g * gd, (g + 1) * gd) for g in range(GMLP_GROUPS)]
    row = lax.broadcasted_iota(jnp.int32, (GMLP_CHUNK, GMLP_CHUNK), 0)
    col = lax.broadcasted_iota(jnp.int32, (GMLP_CHUNK, GMLP_CHUNK), 1)
    w_s = [jnp.where(row >= col, w_s_ref[g], 0.0).astype(BF16) for g in range(GMLP_GROUPS)]

    def gate_values(rows):
        x = h_ref[rows, :]
        xs, scale = _rms_split(x, g_ref[...])
        total = jnp.zeros((GMLP_SUB, 1), F32)
        for cols in groups:
            lo = GMLP_HALF + cols.start
            vg = _gelu(_dot(xs, w_in_ref[:, lo:lo + gd]) * scale)
            v_ref[rows, cols] = vg
            total = total + jnp.sum(vg, axis=-1, keepdims=True)
        mu = total * (1.0 / GMLP_HALF)
        sq = jnp.zeros((GMLP_SUB, 1), F32)
        for cols in groups:
            sq = sq + jnp.sum(jnp.square(v_ref[rows, cols] - mu), axis=-1, keepdims=True)
        return x, xs, scale, mu, lax.rsqrt(sq * (1.0 / GMLP_HALF) + GN_EPS)

    def mix(rows, xs, scale, mu, rstd):
        for g, cols in enumerate(groups):
            vn = ((v_ref[rows, cols] - mu) * rstd * ln_g_ref[:, cols] + ln_b_ref[:, cols]).astype(BF16)
            u = _gelu(_dot(xs, w_in_ref[:, cols]) * scale)
            for ch in range(GMLP_SUB // GMLP_CHUNK):
                sub = slice(ch * GMLP_CHUNK, (ch + 1) * GMLP_CHUNK)
                dst = slice(rows.start + sub.start, rows.start + sub.stop)
                sv = _dot(w_s[g], vn[sub]) + b_s_ref[g]
                prod_ref[dst, cols] = (u[sub] * sv).astype(BF16)

    def project(rows, x):
        o_ref[rows, :] = x + _dot(prod_ref[rows, :], w_out_ref[...])

    subs = [slice(i * GMLP_SUB, (i + 1) * GMLP_SUB) for i in range(GMLP_TILE // GMLP_SUB)]
    stats = [gate_values(rows) for rows in subs]
    for rows, (x, xs, scale, mu, rstd) in zip(subs, stats):
        mix(rows, xs, scale, mu, rstd)
        project(rows, x)


def _gmlp_layer(h, j, g_mix, w_in, ln_g, ln_b, w_s, b_s, w_out):
    tile = pl.BlockSpec((None, GMLP_TILE, D_MODEL), lambda b, s: (b, s, 0))
    return pl.pallas_call(
        functools.partial(_gmlp_kernel, j=j),
        grid=(BATCH, SEQ // GMLP_TILE),
        in_specs=[tile,
                  _const_spec((1, D_MODEL)),
                  _HBM,
                  _const_spec((1, GMLP_HALF)),
                  _const_spec((1, GMLP_HALF)),
                  _const_spec(w_s.shape),
                  _const_spec((GMLP_GROUPS, GMLP_CHUNK, 1)),
                  _HBM],
        out_specs=tile,
        out_shape=jax.ShapeDtypeStruct(h.shape, F32),
        scratch_shapes=[pltpu.VMEM((GMLP_TILE, GMLP_HALF), F32),
                        pltpu.VMEM((GMLP_TILE, GMLP_HALF), BF16)]
        + _weight_scratch((D_MODEL, 2 * GMLP_HALF), (GMLP_HALF, D_MODEL)) + _STAGE_SCRATCH,
        compiler_params=_params(*_SEQUENTIAL),
        name="gmlp_layer",
    )(h, g_mix.reshape(1, D_MODEL), w_in, ln_g.reshape(1, GMLP_HALF), ln_b.reshape(1, GMLP_HALF),
      w_s, b_s[:, :, None], w_out)


def _conv_kernel(h_ref, g_ref, w_in_hbm, k_ref, w_out_hbm, o_ref, y_ref, w_in_ref, w_out_ref,
                 stage_ref, sem_ref, *, j):
    @pl.when(_first_step())
    def _():
        _stage_weights([(w_in_hbm.at[j], w_in_ref), (w_out_hbm.at[j], w_out_ref)],
                       stage_ref, sem_ref)

    @pl.when(pl.program_id(1) == 0)
    def _():
        y_ref[0:SUBLANES, :] = jnp.zeros((SUBLANES, D_MODEL), F32)

    x = h_ref[...]
    xs, scale = _rms_split(x, g_ref[...])
    c_gate = _dot(xs, w_in_ref[:, D_MODEL:2 * D_MODEL]) * scale
    hid = _dot(xs, w_in_ref[:, 2 * D_MODEL:]) * scale
    y_ref[SUBLANES:, :] = c_gate * hid
    z = k_ref[CONV_WIDTH - 1:CONV_WIDTH, :] * y_ref[SUBLANES:, :]
    for lag in range(1, CONV_WIDTH):
        tap = k_ref[CONV_WIDTH - 1 - lag:CONV_WIDTH - lag, :]
        z = z + tap * y_ref[pl.ds(SUBLANES - lag, CONV_TILE), :]
    y_ref[0:SUBLANES, :] = y_ref[CONV_TILE:, :]
    b_gate = _dot(xs, w_in_ref[:, :D_MODEL]) * scale
    o_ref[...] = x + _dot((b_gate * z).astype(BF16), w_out_ref[...])


def _conv_layer(h, j, g_mix, w_in, conv_k, w_out):
    tile = pl.BlockSpec((None, CONV_TILE, D_MODEL), lambda b, s: (b, s, 0))
    return pl.pallas_call(
        functools.partial(_conv_kernel, j=j),
        grid=(BATCH, SEQ // CONV_TILE),
        in_specs=[tile,
                  _const_spec((1, D_MODEL)),
                  _HBM,
                  _const_spec((CONV_WIDTH, D_MODEL)),
                  _HBM],
        out_specs=tile,
        out_shape=jax.ShapeDtypeStruct(h.shape, F32),
        scratch_shapes=[pltpu.VMEM((SUBLANES + CONV_TILE, D_MODEL), F32)]
        + _weight_scratch((D_MODEL, 3 * D_MODEL), (D_MODEL, D_MODEL)) + _STAGE_SCRATCH,
        compiler_params=_params(*_SEQUENTIAL),
        name="conv_layer",
    )(h, g_mix.reshape(1, D_MODEL), w_in, conv_k, w_out)


def _xattn_ffn_kernel(h_ref, gx_ref, w_q_hbm, mem_ref, gm_ref, w_kv_hbm, w_o_hbm, gf_ref, w1_hbm,
                      w2_hbm, gfin_ref, o_ref, w_q_ref, w_kv_ref, w_o_ref, w1_ref, w2_ref, kt_ref,
                      v_ref, stage_ref, sem_ref, *, layer, final_norm):
    @pl.when(_first_step())
    def _():
        _stage_weights([(w_q_hbm.at[layer], w_q_ref), (w_kv_hbm.at[layer], w_kv_ref),
                        (w_o_hbm.at[layer], w_o_ref), (w1_hbm.at[layer], w1_ref),
                        (w2_hbm.at[layer], w2_ref)], stage_ref, sem_ref)

    @pl.when(pl.program_id(1) == 0)
    def _():
        ms, scale = _rms_split(mem_ref[...], gm_ref[...])
        kt_ref[...] = (_dot(ms, w_kv_ref[:, :D_MODEL]) * scale).T.astype(BF16)
        v_ref[...] = (_dot(ms, w_kv_ref[:, D_MODEL:]) * scale).astype(BF16)

    head_cols = [slice(hd * XA_DH, (hd + 1) * XA_DH) for hd in range(XA_HEADS)]

    def query(rows):
        x = h_ref[rows, :]
        xs, scale = _rms_split(x, gx_ref[...])
        return x, (_dot(xs, w_q_ref[...]) * scale).astype(BF16)

    def probs(q):
        out = []
        for cols in head_cols:
            s = _dot(q[:, cols], kt_ref[cols, :]) * (XA_DH ** -0.5)
            e = jnp.exp(s - jnp.max(s, axis=-1, keepdims=True))
            out.append((e / jnp.sum(e, axis=-1, keepdims=True)).astype(BF16))
        return out

    def attend(x, p):
        heads = [_dot(ph, v_ref[:, cols]).astype(BF16) for ph, cols in zip(p, head_cols)]
        return x + _dot(jnp.concatenate(heads, axis=-1), w_o_ref[...])

    def hidden(y):
        ys, scale = _rms_split(y, gf_ref[...])
        acts = []
        for c in range(D_FF // FFN_COLS):
            cols = slice(c * FFN_COLS, (c + 1) * FFN_COLS)
            acts.append(jnp.square(jax.nn.relu(_dot(ys, w1_ref[:, cols]) * scale)).astype(BF16))
        return jnp.concatenate(acts, axis=-1)

    def finish(rows, y, act):
        out = y + _dot(act, w2_ref[...])
        o_ref[rows, :] = _rms(out, gfin_ref[...]) if final_norm else out

    subs = [slice(i * XA_SUB, (i + 1) * XA_SUB) for i in range(XF_TILE // XA_SUB)]
    x, q = query(subs[0])
    p = probs(q)
    for i, rows in enumerate(subs):
        ahead = query(subs[i + 1]) if i + 1 < len(subs) else None
        y = attend(x, p)
        if ahead is not None:
            x, p = ahead[0], probs(ahead[1])
        finish(rows, y, hidden(y))


def _xattn_ffn_layer(h, mem, layer, norm_xa_g, w_q, norm_mem_g, w_kv, w_o, norm_ffn_g, w1, w2,
                     norm_f_g, final_norm):
    tile = pl.BlockSpec((None, XF_TILE, D_MODEL), lambda b, j: (b, j, 0))
    return pl.pallas_call(
        functools.partial(_xattn_ffn_kernel, layer=layer, final_norm=final_norm),
        grid=(BATCH, SEQ // XF_TILE),
        in_specs=[tile,
                  _layer_spec((1, D_MODEL), layer),
                  _HBM,
                  pl.BlockSpec((None, MEM_LEN, D_MODEL), lambda b, j: (b, 0, 0)),
                  _layer_spec((1, D_MODEL), layer),
                  _HBM,
                  _HBM,
                  _layer_spec((1, D_MODEL), layer),
                  _HBM,
                  _HBM,
                  _const_spec((1, D_MODEL))],
        out_specs=tile,
        out_shape=jax.ShapeDtypeStruct(h.shape, F32),
        scratch_shapes=_weight_scratch((D_MODEL, D_MODEL), (D_MODEL, 2 * D_MODEL),
                                       (D_MODEL, D_MODEL), (D_MODEL, D_FF), (D_FF, D_MODEL),
                                       (D_MODEL, MEM_LEN), (MEM_LEN, D_MODEL)) + _STAGE_SCRATCH,
        compiler_params=_params(*_SEQUENTIAL),
        name="xattn_ffn_layer",
    )(h, norm_xa_g.reshape(DEPTH, 1, D_MODEL), w_q, mem, norm_mem_g.reshape(DEPTH, 1, D_MODEL),
      w_kv, w_o, norm_ffn_g.reshape(DEPTH, 1, D_MODEL), w1, w2, norm_f_g.reshape(1, D_MODEL))


def kernel(x, mem, positions, norm_mix_g, norm_xa_g, norm_mem_g, xa_w_q, xa_w_kv, xa_w_o,
           norm_ffn_g, ffn_w1, ffn_w2, ret_w_in, ret_gn_g, ret_w_out,
           gmlp_w_in, gmlp_ln_g, gmlp_ln_b, gmlp_w_s, gmlp_b_s, gmlp_w_out,
           conv_w_in, conv_k, conv_w_out, norm_f_g):
    h = x
    tables = None
    for i in range(DEPTH):
        kind, j = i % N_MIXERS, i // N_MIXERS
        if kind == 0:
            h, tables = _retention_layer(h, tables, positions, i, j, norm_mix_g, ret_w_in,
                                         ret_gn_g, ret_w_out)
        elif kind == 1:
            h = _gmlp_layer(h, j, norm_mix_g[i], gmlp_w_in, gmlp_ln_g[j], gmlp_ln_b[j],
                            gmlp_w_s[j], gmlp_b_s[j], gmlp_w_out)
        else:
            h = _conv_layer(h, j, norm_mix_g[i], conv_w_in, conv_k[j], conv_w_out)
        h = _xattn_ffn_layer(h, mem, i, norm_xa_g, xa_w_q, norm_mem_g, xa_w_kv, xa_w_o, norm_ffn_g,
                             ffn_w1, ffn_w2, norm_f_g, final_norm=(i == DEPTH - 1))
    return h
```

```python
import functools

import jax
import jax.numpy as jnp
from jax import lax
from jax.experimental import pallas as pl
from jax.experimental.pallas import tpu as pltpu

D_MODEL = 1024
BATCH = 8
SEQ = 2048
DEPTH = 4
N_MIXERS = 3

RET_HEADS = 4
RET_DK = D_MODEL // RET_HEADS
RET_DV = 2 * D_MODEL // RET_HEADS
RET_IN = 6 * D_MODEL
ROPE_BASE = 10000.0
ROPE_HALF = RET_DK // 2

GMLP_HALF = 3 * D_MODEL
GMLP_GROUPS = 4
GMLP_GROUP_DIM = GMLP_HALF // GMLP_GROUPS
GMLP_CHUNK = 128

CONV_WIDTH = 3

MEM_LEN = 256
XA_HEADS = 4
XA_DH = D_MODEL // XA_HEADS

D_FF = 4 * D_MODEL

NORM_EPS = 1e-6
GN_EPS = 1e-5

RET_CHUNK = 256
RET_TILE = 512
GMLP_TILE = 512
GMLP_SUB = 256
CONV_TILE = 512
XF_TILE = 512
XA_SUB = 256
TRIG_TILE = 512
FFN_COLS = 1024
OUT_COLS = 512
SUBLANES = 8
STAGE_ROWS, STAGE_COLS = 512, 1024

VMEM_LIMIT_BYTES = 56 * 1024 * 1024

BF16 = jnp.bfloat16
F32 = jnp.float32


def _dot(a, b):
    return jnp.dot(a, b, preferred_element_type=F32)


def _dot_nt(a, b):
    return lax.dot_general(a, b, (((1,), (1,)), ((), ())), preferred_element_type=F32)


def _dot_tn(a, b):
    return lax.dot_general(a, b, (((0,), (0,)), ((), ())), preferred_element_type=F32)


def _rms(x, g):
    return x * lax.rsqrt(jnp.mean(x * x, axis=-1, keepdims=True) + NORM_EPS) * g


def _rms_split(x, g):
    operand = (x * g).astype(BF16)
    scale = lax.rsqrt(jnp.mean(x * x, axis=-1, keepdims=True) + NORM_EPS)
    return operand, scale


def _const_spec(shape):
    zeros = (0,) * len(shape)
    return pl.BlockSpec(shape, lambda *_: zeros, pipeline_mode=pl.Buffered(1))


def _layer_spec(shape, layer):
    idx = (layer,) + (0,) * len(shape)
    return pl.BlockSpec((None,) + shape, lambda *_: idx, pipeline_mode=pl.Buffered(1))


def _params(*semantics):
    return pltpu.CompilerParams(dimension_semantics=semantics,
                                vmem_limit_bytes=VMEM_LIMIT_BYTES)


_SEQUENTIAL = ("arbitrary", "arbitrary")
_HBM = pl.BlockSpec(memory_space=pl.ANY)
_STAGE_SCRATCH = [pltpu.VMEM((2, STAGE_ROWS, STAGE_COLS), F32), pltpu.SemaphoreType.DMA((2,))]


def _weight_scratch(*shapes):
    return [pltpu.VMEM(shape, BF16) for shape in shapes]


N_NEXT = 5
MIXER_STEPS_PER_ROW = 4


def _next_weight_jobs(weights, layer):
    steps = BATCH * MIXER_STEPS_PER_ROW
    args, in_specs, out_specs, out_shapes = [], [], [], []
    for w in weights:
        depth, rows, cols = w.shape
        slab = rows // steps
        assert slab * steps == rows and slab % (2 * SUBLANES) == 0, w.shape
        args.append(w.reshape(depth, steps, slab, cols))
        in_specs.append(pl.BlockSpec((None, None, slab, cols),
                                     lambda b, s: (layer, b * MIXER_STEPS_PER_ROW + s, 0, 0)))
        out_specs.append(pl.BlockSpec((None, slab, cols),
                                      lambda b, s: (b * MIXER_STEPS_PER_ROW + s, 0, 0)))
        out_shapes.append(jax.ShapeDtypeStruct((steps, slab, cols), BF16))
    return args, in_specs, out_specs, out_shapes


def _convert_slabs(src_refs, dst_refs):
    for src, dst in zip(src_refs, dst_refs):
        dst[...] = src[...].astype(BF16)


def _first_step():
    return jnp.logical_and(pl.program_id(0) == 0, pl.program_id(1) == 0)


def _stage_weights(jobs, stage_ref, sem_ref):
    blocks = []
    for src, dst in jobs:
        rows, cols = dst.shape
        assert rows % STAGE_ROWS == 0 and cols % STAGE_COLS == 0, dst.shape
        blocks += [(src, dst, r, c) for r in range(0, rows, STAGE_ROWS)
                   for c in range(0, cols, STAGE_COLS)]

    def fetch(n):
        src, _, r, c = blocks[n]
        return pltpu.make_async_copy(src.at[pl.ds(r, STAGE_ROWS), pl.ds(c, STAGE_COLS)],
                                     stage_ref.at[n % 2], sem_ref.at[n % 2])

    fetch(0).start()
    for n, (_, dst, r, c) in enumerate(blocks):
        if n + 1 < len(blocks):
            fetch(n + 1).start()
        fetch(n).wait()
        dst[r:r + STAGE_ROWS, c:c + STAGE_COLS] = stage_ref[n % 2].astype(BF16)


def _retention_kernel(decay_ref, h_ref, trig_a_ref, trig_b_ref, g_ref, w_in_hbm, intra_ref,
                      xi_ref, zeta_ref, gn_ref, w_out_hbm, *rest, j, make_tables):
    next_src, (o_ref, *rest) = rest[:N_NEXT], rest[N_NEXT:]
    if make_tables:
        cos_out_ref, sin_out_ref, *rest = rest
    next_dst, rest = rest[:N_NEXT], rest[N_NEXT:]
    state_ref, gated_ref, w_in_ref, w_out_ref, stage_ref, sem_ref = rest

    @pl.when(_first_step())
    def _():
        _stage_weights([(w_in_hbm.at[j], w_in_ref), (w_out_hbm.at[j], w_out_ref)],
                       stage_ref, sem_ref)

    @pl.when(pl.program_id(1) == 0)
    def _():
        state_ref[...] = jnp.zeros_like(state_ref)

    x = h_ref[...]
    xs, scale = _rms_split(x, g_ref[...])

    k_off, v_off, gate_off = D_MODEL, 2 * D_MODEL, 4 * D_MODEL
    q_all = _dot(xs, w_in_ref[:, :k_off]) * scale
    k_all = _dot(xs, w_in_ref[:, k_off:v_off]) * (scale * RET_DK ** -0.5)
    if make_tables:
        ang = trig_a_ref[...].astype(F32) * trig_b_ref[...]
        cos_all, sin_all = jnp.cos(ang), jnp.sin(ang)
        cos_out_ref[...] = cos_all
        sin_out_ref[...] = sin_all
    else:
        cos_all, sin_all = trig_a_ref[...], trig_b_ref[...]
    v_all = (_dot(xs, w_in_ref[:, v_off:gate_off]) * scale).astype(BF16)

    def recur(rows, hd, gate):
        cos = cos_all[rows]
        sin = sin_all[rows]

        def rotate(t):
            t1, t2 = t[:, :ROPE_HALF], t[:, ROPE_HALF:]
            return jnp.concatenate([t1 * cos - t2 * sin, t2 * cos + t1 * sin], axis=-1)

        dk = slice(hd * RET_DK, (hd + 1) * RET_DK)
        dv = slice(hd * RET_DV, (hd + 1) * RET_DV)
        q = rotate(q_all[rows, dk]).astype(BF16)
        k = rotate(k_all[rows, dk])
        v = v_all[rows, dv]
        state = state_ref[hd]
        s = _dot_nt(q, k.astype(BF16)) * intra_ref[hd]
        o = _dot(s.astype(BF16), v) + _dot(q, state.astype(BF16)) * xi_ref[hd]
        state_ref[hd] = state * decay_ref[hd] + _dot_tn((k * zeta_ref[hd]).astype(BF16), v)
        mu = jnp.mean(o, axis=-1, keepdims=True)
        var = jnp.mean(jnp.square(o - mu), axis=-1, keepdims=True)
        on = (o - mu) * lax.rsqrt(var + GN_EPS) * gn_ref[:, dv]
        gated_ref[rows, dv] = (jax.nn.silu(gate) * on).astype(BF16)

    chunks = [slice(c * RET_CHUNK, (c + 1) * RET_CHUNK) for c in range(RET_TILE // RET_CHUNK)]
    out_cols = [slice(n * OUT_COLS, (n + 1) * OUT_COLS) for n in range(D_MODEL // OUT_COLS)]
    gates = []
    for hd in range(RET_HEADS):
        lo = gate_off + hd * RET_DV
        gates.append(_dot(xs, w_in_ref[:, lo:lo + RET_DV]) * scale)
        recur(chunks[0], hd, gates[hd][chunks[0]])
    for prev, rows in zip(chunks[:-1], chunks[1:]):
        pieces = []
        for hd in range(RET_HEADS):
            recur(rows, hd, gates[hd][rows])
            if hd < len(out_cols):
                pieces.append(_dot(gated_ref[prev, :], w_out_ref[:, out_cols[hd]]))
        o_ref[prev, :] = x[prev] + jnp.concatenate(pieces, axis=-1)
    last = chunks[-1]
    _convert_slabs(next_src, next_dst)
    o_ref[last, :] = x[last] + _dot(gated_ref[last, :], w_out_ref[...])


def _retention_layer(h, tables, positions, layer, j, norm_mix_g, w_in, gn_g, w_out, next_weights):
    c, t = RET_CHUNK, RET_TILE
    assert SEQ // t == MIXER_STEPS_PER_ROW
    next_args, next_in, next_out, next_shapes = _next_weight_jobs(next_weights, layer)
    log_g = jnp.log(1.0 - 2.0 ** (-5.0 - jnp.arange(RET_HEADS, dtype=F32)))
    idx = jnp.arange(c, dtype=F32)
    rel = idx[:, None] - idx[None, :]
    intra = jnp.where(rel[None] >= 0, jnp.exp(rel[None] * log_g[:, None, None]), 0.0)
    xi = jnp.exp((idx[None] + 1.0) * log_g[:, None])[..., None]
    zeta = jnp.exp((c - 1.0 - idx[None]) * log_g[:, None])[..., None]
    chunk_decay = jnp.exp(c * log_g)

    tile = pl.BlockSpec((None, t, D_MODEL), lambda b, s: (b, s, 0))
    trig = pl.BlockSpec((None, t, ROPE_HALF), lambda b, s: (b, s, 0))
    make_tables = tables is None
    if make_tables:
        inv_freq = ROPE_BASE ** (-jnp.arange(ROPE_HALF, dtype=F32) / ROPE_HALF)
        trig_args = (positions.reshape(BATCH, SEQ, 1), inv_freq.reshape(1, ROPE_HALF))
        trig_specs = [pl.BlockSpec((None, t, 1), lambda b, s: (b, s, 0)),
                      _const_spec((1, ROPE_HALF))]
        table_shape = jax.ShapeDtypeStruct((BATCH, SEQ, ROPE_HALF), F32)
        out_specs = [tile, trig, trig]
        out_shape = [jax.ShapeDtypeStruct(h.shape, F32), table_shape, table_shape]
    else:
        trig_args, trig_specs = tuple(tables), [trig, trig]
        out_specs, out_shape = [tile], [jax.ShapeDtypeStruct(h.shape, F32)]
    out = pl.pallas_call(
        functools.partial(_retention_kernel, j=j, make_tables=make_tables),
        grid=(BATCH, SEQ // t),
        in_specs=[pl.BlockSpec(memory_space=pltpu.SMEM),
                  tile, *trig_specs,
                  _layer_spec((1, D_MODEL), layer),
                  _HBM,
                  _const_spec((RET_HEADS, c, c)),
                  _const_spec((RET_HEADS, c, 1)),
                  _const_spec((RET_HEADS, c, 1)),
                  _layer_spec((1, 2 * D_MODEL), j),
                  _HBM,
                  *next_in],
        out_specs=out_specs + next_out,
        out_shape=out_shape + next_shapes,
        scratch_shapes=[pltpu.VMEM((RET_HEADS, RET_DK, RET_DV), F32),
                        pltpu.VMEM((t, 2 * D_MODEL), BF16)]
        + _weight_scratch((D_MODEL, RET_IN), (2 * D_MODEL, D_MODEL)) + _STAGE_SCRATCH,
        compiler_params=_params(*_SEQUENTIAL),
        name="retention_layer",
    )(chunk_decay, h, *trig_args, norm_mix_g.reshape(DEPTH, 1, D_MODEL), w_in, intra, xi, zeta,
      gn_g.reshape(-1, 1, 2 * D_MODEL), w_out, *next_args)
    own, converted = out[:-N_NEXT], out[-N_NEXT:]
    return own[0], (tuple(own[1:]) if make_tables else tuple(tables)), converted


def _gelu(x):
    return 0.5 * x * (1.0 + lax.erf(x * (2.0 ** -0.5)))


def _gmlp_kernel(h_ref, g_ref, w_in_hbm, ln_g_ref, ln_b_ref, w_s_ref, b_s_ref, w_out_hbm,
                 *rest, j):
    next_src, (o_ref, *rest) = rest[:N_NEXT], rest[N_NEXT:]
    next_dst, rest = rest[:N_NEXT], rest[N_NEXT:]
    v_ref, prod_ref, w_in_ref, w_out_ref, stage_ref, sem_ref = rest

    @pl.when(_first_step())
    def _():
        _stage_weights([(w_in_hbm.at[j], w_in_ref), (w_out_hbm.at[j], w_out_ref)],
                       stage_ref, sem_ref)

    gd = GMLP_GROUP_DIM
    groups = [slice(g * gd, (g + 1) * gd) for g in range(GMLP_GROUPS)]
    row = lax.broadcasted_iota(jnp.int32, (GMLP_CHUNK, GMLP_CHUNK), 0)
    col = lax.broadcasted_iota(jnp.int32, (GMLP_CHUNK, GMLP_CHUNK), 1)
    w_s = [jnp.where(row >= col, w_s_ref[g], 0.0).astype(BF16) for g in range(GMLP_GROUPS)]

    def gate_values(rows):
        x = h_ref[rows, :]
        xs, scale = _rms_split(x, g_ref[...])
        total = jnp.zeros((GMLP_SUB, 1), F32)
        for cols in groups:
            lo = GMLP_HALF + cols.start
            vg = _gelu(_dot(xs, w_in_ref[:, lo:lo + gd]) * scale)
            v_ref[rows, cols] = vg
            total = total + jnp.sum(vg, axis=-1, keepdims=True)
        mu = total * (1.0 / GMLP_HALF)
        sq = jnp.zeros((GMLP_SUB, 1), F32)
        for cols in groups:
            sq = sq + jnp.sum(jnp.square(v_ref[rows, cols] - mu), axis=-1, keepdims=True)
        return x, xs, scale, mu, lax.rsqrt(sq * (1.0 / GMLP_HALF) + GN_EPS)

    def mix(rows, xs, scale, mu, rstd):
        for g, cols in enumerate(groups):
            vn = ((v_ref[rows, cols] - mu) * rstd * ln_g_ref[:, cols] + ln_b_ref[:, cols]).astype(BF16)
            u = _gelu(_dot(xs, w_in_ref[:, cols]) * scale)
            for ch in range(GMLP_SUB // GMLP_CHUNK):
                sub = slice(ch * GMLP_CHUNK, (ch + 1) * GMLP_CHUNK)
                dst = slice(rows.start + sub.start, rows.start + sub.stop)
                sv = _dot(w_s[g], vn[sub]) + b_s_ref[g]
                prod_ref[dst, cols] = (u[sub] * sv).astype(BF16)

    def project(rows, x):
        o_ref[rows, :] = x + _dot(prod_ref[rows, :], w_out_ref[...])

    subs = [slice(i * GMLP_SUB, (i + 1) * GMLP_SUB) for i in range(GMLP_TILE // GMLP_SUB)]
    stats = [gate_values(rows) for rows in subs]
    for rows, (x, xs, scale, mu, rstd) in zip(subs, stats):
        mix(rows, xs, scale, mu, rstd)
        if rows is subs[-1]:
            _convert_slabs(next_src, next_dst)
        project(rows, x)


def _gmlp_layer(h, layer, j, g_mix, w_in, ln_g, ln_b, w_s, b_s, w_out, next_weights):
    assert SEQ // GMLP_TILE == MIXER_STEPS_PER_ROW
    next_args, next_in, next_out, next_shapes = _next_weight_jobs(next_weights, layer)
    tile = pl.BlockSpec((None, GMLP_TILE, D_MODEL), lambda b, s: (b, s, 0))
    out = pl.pallas_call(
        functools.partial(_gmlp_kernel, j=j),
        grid=(BATCH, SEQ // GMLP_TILE),
        in_specs=[tile,
                  _const_spec((1, D_MODEL)),
                  _HBM,
                  _const_spec((1, GMLP_HALF)),
                  _const_spec((1, GMLP_HALF)),
                  _const_spec(w_s.shape),
                  _const_spec((GMLP_GROUPS, GMLP_CHUNK, 1)),
                  _HBM,
                  *next_in],
        out_specs=[tile] + next_out,
        out_shape=[jax.ShapeDtypeStruct(h.shape, F32)] + next_shapes,
        scratch_shapes=[pltpu.VMEM((GMLP_TILE, GMLP_HALF), F32),
                        pltpu.VMEM((GMLP_TILE, GMLP_HALF), BF16)]
        + _weight_scratch((D_MODEL, 2 * GMLP_HALF), (GMLP_HALF, D_MODEL)) + _STAGE_SCRATCH,
        compiler_params=_params(*_SEQUENTIAL),
        name="gmlp_layer",
    )(h, g_mix.reshape(1, D_MODEL), w_in, ln_g.reshape(1, GMLP_HALF), ln_b.reshape(1, GMLP_HALF),
      w_s, b_s[:, :, None], w_out, *next_args)
    return out[0], out[1:]


def _conv_kernel(h_ref, g_ref, w_in_hbm, k_ref, w_out_hbm, *rest, j):
    next_src, (o_ref, *rest) = rest[:N_NEXT], rest[N_NEXT:]
    next_dst, rest = rest[:N_NEXT], rest[N_NEXT:]
    y_ref, w_in_ref, w_out_ref, stage_ref, sem_ref = rest

    @pl.when(_first_step())
    def _():
        _stage_weights([(w_in_hbm.at[j], w_in_ref), (w_out_hbm.at[j], w_out_ref)],
                       stage_ref, sem_ref)

    @pl.when(pl.program_id(1) == 0)
    def _():
        y_ref[0:SUBLANES, :] = jnp.zeros((SUBLANES, D_MODEL), F32)

    x = h_ref[...]
    xs, scale = _rms_split(x, g_ref[...])
    c_gate = _dot(xs, w_in_ref[:, D_MODEL:2 * D_MODEL]) * scale
    hid = _dot(xs, w_in_ref[:, 2 * D_MODEL:]) * scale
    y_ref[SUBLANES:, :] = c_gate * hid
    z = k_ref[CONV_WIDTH - 1:CONV_WIDTH, :] * y_ref[SUBLANES:, :]
    for lag in range(1, CONV_WIDTH):
        tap = k_ref[CONV_WIDTH - 1 - lag:CONV_WIDTH - lag, :]
        z = z + tap * y_ref[pl.ds(SUBLANES - lag, CONV_TILE), :]
    y_ref[0:SUBLANES, :] = y_ref[CONV_TILE:, :]
    b_gate = _dot(xs, w_in_ref[:, :D_MODEL]) * scale
    _convert_slabs(next_src, next_dst)
    o_ref[...] = x + _dot((b_gate * z).astype(BF16), w_out_ref[...])


def _conv_layer(h, layer, j, g_mix, w_in, conv_k, w_out, next_weights):
    assert SEQ // CONV_TILE == MIXER_STEPS_PER_ROW
    next_args, next_in, next_out, next_shapes = _next_weight_jobs(next_weights, layer)
    tile = pl.BlockSpec((None, CONV_TILE, D_MODEL), lambda b, s: (b, s, 0))
    out = pl.pallas_call(
        functools.partial(_conv_kernel, j=j),
        grid=(BATCH, SEQ // CONV_TILE),
        in_specs=[tile,
                  _const_spec((1, D_MODEL)),
                  _HBM,
                  _const_spec((CONV_WIDTH, D_MODEL)),
                  _HBM,
                  *next_in],
        out_specs=[tile] + next_out,
        out_shape=[jax.ShapeDtypeStruct(h.shape, F32)] + next_shapes,
        scratch_shapes=[pltpu.VMEM((SUBLANES + CONV_TILE, D_MODEL), F32)]
        + _weight_scratch((D_MODEL, 3 * D_MODEL), (D_MODEL, D_MODEL)) + _STAGE_SCRATCH,
        compiler_params=_params(*_SEQUENTIAL),
        name="conv_layer",
    )(h, g_mix.reshape(1, D_MODEL), w_in, conv_k, w_out, *next_args)
    return out[0], out[1:]


def _xattn_ffn_kernel(h_ref, gx_ref, w_q_ref, mem_ref, gm_ref, w_kv_ref, w_o_ref, gf_ref, w1_ref,
                      w2_ref, gfin_ref, o_ref, kt_ref, v_ref, *, final_norm):
    @pl.when(pl.program_id(1) == 0)
    def _():
        ms, scale = _rms_split(mem_ref[...], gm_ref[...])
        kt_ref[...] = (_dot(ms, w_kv_ref[:, :D_MODEL]) * scale).T.astype(BF16)
        v_ref[...] = (_dot(ms, w_kv_ref[:, D_MODEL:]) * scale).astype(BF16)

    head_cols = [slice(hd * XA_DH, (hd + 1) * XA_DH) for hd in range(XA_HEADS)]

    def query(rows):
        x = h_ref[rows, :]
        xs, scale = _rms_split(x, gx_ref[...])
        return x, (_dot(xs, w_q_ref[...]) * scale).astype(BF16)

    def probs(q):
        out = []
        for cols in head_cols:
            s = _dot(q[:, cols], kt_ref[cols, :]) * (XA_DH ** -0.5)
            e = jnp.exp(s - jnp.max(s, axis=-1, keepdims=True))
            out.append((e / jnp.sum(e, axis=-1, keepdims=True)).astype(BF16))
        return out

    def attend(x, p):
        heads = [_dot(ph, v_ref[:, cols]).astype(BF16) for ph, cols in zip(p, head_cols)]
        return x + _dot(jnp.concatenate(heads, axis=-1), w_o_ref[...])

    def hidden(y):
        ys, scale = _rms_split(y, gf_ref[...])
        acts = []
        for c in range(D_FF // FFN_COLS):
            cols = slice(c * FFN_COLS, (c + 1) * FFN_COLS)
            acts.append(jnp.square(jax.nn.relu(_dot(ys, w1_ref[:, cols]) * scale)).astype(BF16))
        return jnp.concatenate(acts, axis=-1)

    def finish(rows, y, act):
        out = y + _dot(act, w2_ref[...])
        o_ref[rows, :] = _rms(out, gfin_ref[...]) if final_norm else out

    subs = [slice(i * XA_SUB, (i + 1) * XA_SUB) for i in range(XF_TILE // XA_SUB)]
    x, q = query(subs[0])
    p = probs(q)
    for i, rows in enumerate(subs):
        ahead = query(subs[i + 1]) if i + 1 < len(subs) else None
        y = attend(x, p)
        if ahead is not None:
            x, p = ahead[0], probs(ahead[1])
        finish(rows, y, hidden(y))


def _xattn_ffn_layer(h, mem, layer, norm_xa_g, norm_mem_g, norm_ffn_g, norm_f_g, weights,
                     final_norm):
    w_q, w_kv, w_o, w1, w2 = (w.reshape(-1, w.shape[-1]) for w in weights)
    tile = pl.BlockSpec((None, XF_TILE, D_MODEL), lambda b, j: (b, j, 0))
    return pl.pallas_call(
        functools.partial(_xattn_ffn_kernel, final_norm=final_norm),
        grid=(BATCH, SEQ // XF_TILE),
        in_specs=[tile,
                  _layer_spec((1, D_MODEL), layer),
                  _const_spec(w_q.shape),
                  pl.BlockSpec((None, MEM_LEN, D_MODEL), lambda b, j: (b, 0, 0)),
                  _layer_spec((1, D_MODEL), layer),
                  _const_spec(w_kv.shape),
                  _const_spec(w_o.shape),
                  _layer_spec((1, D_MODEL), layer),
                  _const_spec(w1.shape),
                  _const_spec(w2.shape),
                  _const_spec((1, D_MODEL))],
        out_specs=tile,
        out_shape=jax.ShapeDtypeStruct(h.shape, F32),
        scratch_shapes=_weight_scratch((D_MODEL, MEM_LEN), (MEM_LEN, D_MODEL)),
        compiler_params=_params(*_SEQUENTIAL),
        name="xattn_ffn_layer",
    )(h, norm_xa_g.reshape(DEPTH, 1, D_MODEL), w_q, mem, norm_mem_g.reshape(DEPTH, 1, D_MODEL),
      w_kv, w_o, norm_ffn_g.reshape(DEPTH, 1, D_MODEL), w1, w2, norm_f_g.reshape(1, D_MODEL))


def kernel(x, mem, positions, norm_mix_g, norm_xa_g, norm_mem_g, xa_w_q, xa_w_kv, xa_w_o,
           norm_ffn_g, ffn_w1, ffn_w2, ret_w_in, ret_gn_g, ret_w_out,
           gmlp_w_in, gmlp_ln_g, gmlp_ln_b, gmlp_w_s, gmlp_b_s, gmlp_w_out,
           conv_w_in, conv_k, conv_w_out, norm_f_g):
    h = x
    tables = None
    next_weights = (xa_w_q, xa_w_kv, xa_w_o, ffn_w1, ffn_w2)
    for i in range(DEPTH):
        kind, j = i % N_MIXERS, i // N_MIXERS
        if kind == 0:
            h, tables, converted = _retention_layer(h, tables, positions, i, j, norm_mix_g,
                                                    ret_w_in, ret_gn_g, ret_w_out, next_weights)
        elif kind == 1:
            h, converted = _gmlp_layer(h, i, j, norm_mix_g[i], gmlp_w_in, gmlp_ln_g[j],
                                       gmlp_ln_b[j], gmlp_w_s[j], gmlp_b_s[j], gmlp_w_out,
                                       next_weights)
        else:
            h, converted = _conv_layer(h, i, j, norm_mix_g[i], conv_w_in, conv_k[j], conv_w_out,
                                       next_weights)
        h = _xattn_ffn_layer(h, mem, i, norm_xa_g, norm_mem_g, norm_ffn_g, norm_f_g, converted,
                             final_norm=(i == DEPTH - 1))
    return h
```

```python
import functools

import jax
import jax.numpy as jnp
from jax import lax
from jax.experimental import pallas as pl
from jax.experimental.pallas import tpu as pltpu

D_MODEL = 1024
BATCH = 8
SEQ = 2048
DEPTH = 4
N_MIXERS = 3

RET_HEADS = 4
RET_DK = D_MODEL // RET_HEADS
RET_DV = 2 * D_MODEL // RET_HEADS
RET_IN = 6 * D_MODEL
ROPE_BASE = 10000.0
ROPE_HALF = RET_DK // 2

GMLP_HALF = 3 * D_MODEL
GMLP_GROUPS = 4
GMLP_GROUP_DIM = GMLP_HALF // GMLP_GROUPS
GMLP_CHUNK = 128

CONV_WIDTH = 3

MEM_LEN = 256
XA_HEADS = 4
XA_DH = D_MODEL // XA_HEADS

D_FF = 4 * D_MODEL

NORM_EPS = 1e-6
GN_EPS = 1e-5

RET_CHUNK = 256
RET_TILE = 512
GMLP_TILE = 512
GMLP_SUB = 256
CONV_TILE = 512
XF_TILE = 512
XA_SUB = 256
TRIG_TILE = 512
FFN_COLS = 1024
OUT_COLS = 512
SUBLANES = 8
STAGE_ROWS, STAGE_COLS = 512, 1024

VMEM_LIMIT_BYTES = 56 * 1024 * 1024

BF16 = jnp.bfloat16
F32 = jnp.float32


def _dot(a, b):
    return jnp.dot(a, b, preferred_element_type=F32)


def _dot_nt(a, b):
    return lax.dot_general(a, b, (((1,), (1,)), ((), ())), preferred_element_type=F32)


def _dot_tn(a, b):
    return lax.dot_general(a, b, (((0,), (0,)), ((), ())), preferred_element_type=F32)


def _rms(x, g):
    return x * lax.rsqrt(jnp.mean(x * x, axis=-1, keepdims=True) + NORM_EPS) * g


def _rms_split(x, g):
    operand = (x * g).astype(BF16)
    scale = lax.rsqrt(jnp.mean(x * x, axis=-1, keepdims=True) + NORM_EPS)
    return operand, scale


def _const_spec(shape):
    zeros = (0,) * len(shape)
    return pl.BlockSpec(shape, lambda *_: zeros, pipeline_mode=pl.Buffered(1))


def _layer_spec(shape, layer):
    idx = (layer,) + (0,) * len(shape)
    return pl.BlockSpec((None,) + shape, lambda *_: idx, pipeline_mode=pl.Buffered(1))


def _params(*semantics):
    return pltpu.CompilerParams(dimension_semantics=semantics,
                                vmem_limit_bytes=VMEM_LIMIT_BYTES)


_SEQUENTIAL = ("arbitrary", "arbitrary")
_HBM = pl.BlockSpec(memory_space=pl.ANY)
_STAGE_SCRATCH = [pltpu.VMEM((2, STAGE_ROWS, STAGE_COLS), F32), pltpu.SemaphoreType.DMA((2,))]


def _weight_scratch(*shapes):
    return [pltpu.VMEM(shape, BF16) for shape in shapes]


N_NEXT = 5
MIXER_STEPS_PER_ROW = 4


def _next_weight_jobs(weights, layer):
    steps = BATCH * MIXER_STEPS_PER_ROW
    args, in_specs, out_specs, out_shapes = [], [], [], []
    for w in weights:
        depth, rows, cols = w.shape
        slab = rows // steps
        assert slab * steps == rows and slab % (2 * SUBLANES) == 0, w.shape
        args.append(w.reshape(depth, steps, slab, cols))
        in_specs.append(pl.BlockSpec((None, None, slab, cols),
                                     lambda b, s: (layer, b * MIXER_STEPS_PER_ROW + s, 0, 0)))
        out_specs.append(pl.BlockSpec((None, slab, cols),
                                      lambda b, s: (b * MIXER_STEPS_PER_ROW + s, 0, 0)))
        out_shapes.append(jax.ShapeDtypeStruct((steps, slab, cols), BF16))
    return args, in_specs, out_specs, out_shapes


def _convert_slabs(src_refs, dst_refs):
    for src, dst in zip(src_refs, dst_refs):
        dst[...] = src[...].astype(BF16)


def _first_step():
    return jnp.logical_and(pl.program_id(0) == 0, pl.program_id(1) == 0)


def _stage_weights(jobs, stage_ref, sem_ref):
    blocks = []
    for src, dst in jobs:
        rows, cols = dst.shape
        assert rows % STAGE_ROWS == 0 and cols % STAGE_COLS == 0, dst.shape
        blocks += [(src, dst, r, c) for r in range(0, rows, STAGE_ROWS)
                   for c in range(0, cols, STAGE_COLS)]

    def fetch(n):
        src, _, r, c = blocks[n]
        return pltpu.make_async_copy(src.at[pl.ds(r, STAGE_ROWS), pl.ds(c, STAGE_COLS)],
                                     stage_ref.at[n % 2], sem_ref.at[n % 2])

    fetch(0).start()
    for n, (_, dst, r, c) in enumerate(blocks):
        if n + 1 < len(blocks):
            fetch(n + 1).start()
        fetch(n).wait()
        dst[r:r + STAGE_ROWS, c:c + STAGE_COLS] = stage_ref[n % 2].astype(BF16)


def _retention_kernel(decay_ref, h_ref, trig_a_ref, trig_b_ref, g_ref, w_in_hbm, intra_ref,
                      xi_ref, zeta_ref, gn_ref, w_out_hbm, *rest, j, make_tables):
    next_src, (o_ref, *rest) = rest[:N_NEXT], rest[N_NEXT:]
    if make_tables:
        cos_out_ref, sin_out_ref, *rest = rest
    next_dst, rest = rest[:N_NEXT], rest[N_NEXT:]
    state_ref, gated_ref, w_in_ref, w_out_ref, stage_ref, sem_ref = rest

    @pl.when(_first_step())
    def _():
        _stage_weights([(w_in_hbm.at[j], w_in_ref), (w_out_hbm.at[j], w_out_ref)],
                       stage_ref, sem_ref)

    @pl.when(pl.program_id(1) == 0)
    def _():
        state_ref[...] = jnp.zeros_like(state_ref)

    x = h_ref[...]
    xs, scale = _rms_split(x, g_ref[...])

    k_off, v_off, gate_off = D_MODEL, 2 * D_MODEL, 4 * D_MODEL
    if make_tables:
        ang = trig_a_ref[...].astype(F32) * trig_b_ref[...]
        cos_all, sin_all = jnp.cos(ang), jnp.sin(ang)
        cos_out_ref[...] = cos_all
        sin_out_ref[...] = sin_all
    else:
        cos_all, sin_all = trig_a_ref[...], trig_b_ref[...]

    chunks = [slice(c * RET_CHUNK, (c + 1) * RET_CHUNK) for c in range(RET_TILE // RET_CHUNK)]
    out_cols = [slice(n * OUT_COLS, (n + 1) * OUT_COLS) for n in range(D_MODEL // OUT_COLS)]
    half_v = (gate_off - v_off) // 2
    proj = {}

    def project_part(c, part):
        rows = chunks[c]
        if part == 0:
            proj[c, "q"] = _dot(xs[rows], w_in_ref[:, :k_off]) * scale[rows]
        elif part == 1:
            proj[c, "k"] = _dot(xs[rows], w_in_ref[:, k_off:v_off]) * (scale[rows] * RET_DK ** -0.5)
        else:
            lo = v_off + (part - 2) * half_v
            proj[c, "v", part - 2] = (_dot(xs[rows], w_in_ref[:, lo:lo + half_v])
                                      * scale[rows]).astype(BF16)

    def gate_proj(c, hd):
        lo = gate_off + hd * RET_DV
        return _dot(xs[chunks[c]], w_in_ref[:, lo:lo + RET_DV]) * scale[chunks[c]]

    def recur(c, hd, extras):
        rows = chunks[c]
        cos = cos_all[rows]
        sin = sin_all[rows]

        def rotate(t):
            t1, t2 = t[:, :ROPE_HALF], t[:, ROPE_HALF:]
            return jnp.concatenate([t1 * cos - t2 * sin, t2 * cos + t1 * sin], axis=-1)

        dk = slice(hd * RET_DK, (hd + 1) * RET_DK)
        dv = slice(hd * RET_DV, (hd + 1) * RET_DV)
        heads_per_half = RET_HEADS // 2
        v_cols = slice((hd % heads_per_half) * RET_DV, (hd % heads_per_half + 1) * RET_DV)
        q = rotate(proj[c, "q"][:, dk]).astype(BF16)
        k = rotate(proj[c, "k"][:, dk])
        v = proj[c, "v", hd // heads_per_half][:, v_cols]
        state = state_ref[hd]
        s = _dot_nt(q, k.astype(BF16)) * intra_ref[hd]
        gate = gate_proj(c, hd)
        for extra in extras:
            extra()
        o = _dot(s.astype(BF16), v) + _dot(q, state.astype(BF16)) * xi_ref[hd]
        state_ref[hd] = state * decay_ref[hd] + _dot_tn((k * zeta_ref[hd]).astype(BF16), v)
        mu = jnp.mean(o, axis=-1, keepdims=True)
        var = jnp.mean(jnp.square(o - mu), axis=-1, keepdims=True)
        on = (o - mu) * lax.rsqrt(var + GN_EPS) * gn_ref[:, dv]
        gated_ref[rows, dv] = (jax.nn.silu(gate) * on).astype(BF16)

    for part in range(RET_HEADS):
        project_part(0, part)
    for c in range(len(chunks)):
        pieces = []
        for hd in range(RET_HEADS):
            extras = []
            if c + 1 < len(chunks):
                extras.append(functools.partial(project_part, c + 1, hd))
            if c > 0 and hd < len(out_cols):
                extras.append(lambda cols=out_cols[hd]: pieces.append(
                    _dot(gated_ref[chunks[c - 1], :], w_out_ref[:, cols])))
            recur(c, hd, extras)
        if c > 0:
            o_ref[chunks[c - 1], :] = x[chunks[c - 1]] + jnp.concatenate(pieces, axis=-1)
    last = chunks[-1]
    _convert_slabs(next_src, next_dst)
    o_ref[last, :] = x[last] + _dot(gated_ref[last, :], w_out_ref[...])


def _retention_layer(h, tables, positions, layer, j, norm_mix_g, w_in, gn_g, w_out, next_weights):
    c, t = RET_CHUNK, RET_TILE
    assert SEQ // t == MIXER_STEPS_PER_ROW
    next_args, next_in, next_out, next_shapes = _next_weight_jobs(next_weights, layer)
    log_g = jnp.log(1.0 - 2.0 ** (-5.0 - jnp.arange(RET_HEADS, dtype=F32)))
    idx = jnp.arange(c, dtype=F32)
    rel = idx[:, None] - idx[None, :]
    intra = jnp.where(rel[None] >= 0, jnp.exp(rel[None] * log_g[:, None, None]), 0.0)
    xi = jnp.exp((idx[None] + 1.0) * log_g[:, None])[..., None]
    zeta = jnp.exp((c - 1.0 - idx[None]) * log_g[:, None])[..., None]
    chunk_decay = jnp.exp(c * log_g)

    tile = pl.BlockSpec((None, t, D_MODEL), lambda b, s: (b, s, 0))
    trig = pl.BlockSpec((None, t, ROPE_HALF), lambda b, s: (b, s, 0))
    make_tables = tables is None
    if make_tables:
        inv_freq = ROPE_BASE ** (-jnp.arange(ROPE_HALF, dtype=F32) / ROPE_HALF)
        trig_args = (positions.reshape(BATCH, SEQ, 1), inv_freq.reshape(1, ROPE_HALF))
        trig_specs = [pl.BlockSpec((None, t, 1), lambda b, s: (b, s, 0)),
                      _const_spec((1, ROPE_HALF))]
        table_shape = jax.ShapeDtypeStruct((BATCH, SEQ, ROPE_HALF), F32)
        out_specs = [tile, trig, trig]
        out_shape = [jax.ShapeDtypeStruct(h.shape, F32), table_shape, table_shape]
    else:
        trig_args, trig_specs = tuple(tables), [trig, trig]
        out_specs, out_shape = [tile], [jax.ShapeDtypeStruct(h.shape, F32)]
    out = pl.pallas_call(
        functools.partial(_retention_kernel, j=j, make_tables=make_tables),
        grid=(BATCH, SEQ // t),
        in_specs=[pl.BlockSpec(memory_space=pltpu.SMEM),
                  tile, *trig_specs,
                  _layer_spec((1, D_MODEL), layer),
                  _HBM,
                  _const_spec((RET_HEADS, c, c)),
                  _const_spec((RET_HEADS, c, 1)),
                  _const_spec((RET_HEADS, c, 1)),
                  _layer_spec((1, 2 * D_MODEL), j),
                  _HBM,
                  *next_in],
        out_specs=out_specs + next_out,
        out_shape=out_shape + next_shapes,
        scratch_shapes=[pltpu.VMEM((RET_HEADS, RET_DK, RET_DV), F32),
                        pltpu.VMEM((t, 2 * D_MODEL), BF16)]
        + _weight_scratch((D_MODEL, RET_IN), (2 * D_MODEL, D_MODEL)) + _STAGE_SCRATCH,
        compiler_params=_params(*_SEQUENTIAL),
        name="retention_layer",
    )(chunk_decay, h, *trig_args, norm_mix_g.reshape(DEPTH, 1, D_MODEL), w_in, intra, xi, zeta,
      gn_g.reshape(-1, 1, 2 * D_MODEL), w_out, *next_args)
    own, converted = out[:-N_NEXT], out[-N_NEXT:]
    return own[0], (tuple(own[1:]) if make_tables else tuple(tables)), converted


def _gelu(x):
    return 0.5 * x * (1.0 + lax.erf(x * (2.0 ** -0.5)))


def _gmlp_kernel(h_ref, g_ref, w_in_hbm, ln_g_ref, ln_b_ref, w_s_ref, b_s_ref, w_out_hbm,
                 *rest, j):
    next_src, (o_ref, *rest) = rest[:N_NEXT], rest[N_NEXT:]
    next_dst, rest = rest[:N_NEXT], rest[N_NEXT:]
    v_ref, prod_ref, w_in_ref, w_out_ref, stage_ref, sem_ref = rest

    @pl.when(_first_step())
    def _():
        _stage_weights([(w_in_hbm.at[j], w_in_ref), (w_out_hbm.at[j], w_out_ref)],
                       stage_ref, sem_ref)

    gd = GMLP_GROUP_DIM
    groups = [slice(g * gd, (g + 1) * gd) for g in range(GMLP_GROUPS)]
    row = lax.broadcasted_iota(jnp.int32, (GMLP_CHUNK, GMLP_CHUNK), 0)
    col = lax.broadcasted_iota(jnp.int32, (GMLP_CHUNK, GMLP_CHUNK), 1)
    w_s = [jnp.where(row >= col, w_s_ref[g], 0.0).astype(BF16) for g in range(GMLP_GROUPS)]

    def gate_values(rows):
        x = h_ref[rows, :]
        xs, scale = _rms_split(x, g_ref[...])
        total = jnp.zeros((GMLP_SUB, 1), F32)
        for cols in groups:
            lo = GMLP_HALF + cols.start
            vg = _gelu(_dot(xs, w_in_ref[:, lo:lo + gd]) * scale)
            v_ref[rows, cols] = vg
            total = total + jnp.sum(vg, axis=-1, keepdims=True)
        mu = total * (1.0 / GMLP_HALF)
        sq = jnp.zeros((GMLP_SUB, 1), F32)
        for cols in groups:
            sq = sq + jnp.sum(jnp.square(v_ref[rows, cols] - mu), axis=-1, keepdims=True)
        return x, xs, scale, mu, lax.rsqrt(sq * (1.0 / GMLP_HALF) + GN_EPS)

    def mix(rows, xs, scale, mu, rstd):
        u = _gelu(_dot(xs, w_in_ref[:, groups[0]]) * scale)
        for g, cols in enumerate(groups):
            vn = ((v_ref[rows, cols] - mu) * rstd * ln_g_ref[:, cols] + ln_b_ref[:, cols]).astype(BF16)
            chunk_rows = [slice(ch * GMLP_CHUNK, (ch + 1) * GMLP_CHUNK)
                          for ch in range(GMLP_SUB // GMLP_CHUNK)]
            svs = [_dot(w_s[g], vn[sub]) + b_s_ref[g] for sub in chunk_rows]
            u_next = (_gelu(_dot(xs, w_in_ref[:, groups[g + 1]]) * scale)
                      if g + 1 < len(groups) else None)
            for sub, sv in zip(chunk_rows, svs):
                dst = slice(rows.start + sub.start, rows.start + sub.stop)
                prod_ref[dst, cols] = (u[sub] * sv).astype(BF16)
            u = u_next

    def project(rows, x):
        o_ref[rows, :] = x + _dot(prod_ref[rows, :], w_out_ref[...])

    subs = [slice(i * GMLP_SUB, (i + 1) * GMLP_SUB) for i in range(GMLP_TILE // GMLP_SUB)]
    stats = [gate_values(rows) for rows in subs]
    for rows, (x, xs, scale, mu, rstd) in zip(subs, stats):
        mix(rows, xs, scale, mu, rstd)
        if rows is subs[-1]:
            _convert_slabs(next_src, next_dst)
        project(rows, x)


def _gmlp_layer(h, layer, j, g_mix, w_in, ln_g, ln_b, w_s, b_s, w_out, next_weights):
    assert SEQ // GMLP_TILE == MIXER_STEPS_PER_ROW
    next_args, next_in, next_out, next_shapes = _next_weight_jobs(next_weights, layer)
    tile = pl.BlockSpec((None, GMLP_TILE, D_MODEL), lambda b, s: (b, s, 0))
    out = pl.pallas_call(
        functools.partial(_gmlp_kernel, j=j),
        grid=(BATCH, SEQ // GMLP_TILE),
        in_specs=[tile,
                  _const_spec((1, D_MODEL)),
                  _HBM,
                  _const_spec((1, GMLP_HALF)),
                  _const_spec((1, GMLP_HALF)),
                  _const_spec(w_s.shape),
                  _const_spec((GMLP_GROUPS, GMLP_CHUNK, 1)),
                  _HBM,
                  *next_in],
        out_specs=[tile] + next_out,
        out_shape=[jax.ShapeDtypeStruct(h.shape, F32)] + next_shapes,
        scratch_shapes=[pltpu.VMEM((GMLP_TILE, GMLP_HALF), F32),
                        pltpu.VMEM((GMLP_TILE, GMLP_HALF), BF16)]
        + _weight_scratch((D_MODEL, 2 * GMLP_HALF), (GMLP_HALF, D_MODEL)) + _STAGE_SCRATCH,
        compiler_params=_params(*_SEQUENTIAL),
        name="gmlp_layer",
    )(h, g_mix.reshape(1, D_MODEL), w_in, ln_g.reshape(1, GMLP_HALF), ln_b.reshape(1, GMLP_HALF),
      w_s, b_s[:, :, None], w_out, *next_args)
    return out[0], out[1:]


def _conv_kernel(h_ref, g_ref, w_in_hbm, k_ref, w_out_hbm, *rest, j):
    next_src, (o_ref, *rest) = rest[:N_NEXT], rest[N_NEXT:]
    next_dst, rest = rest[:N_NEXT], rest[N_NEXT:]
    y_ref, w_in_ref, w_out_ref, stage_ref, sem_ref = rest

    @pl.when(_first_step())
    def _():
        _stage_weights([(w_in_hbm.at[j], w_in_ref), (w_out_hbm.at[j], w_out_ref)],
                       stage_ref, sem_ref)

    @pl.when(pl.program_id(1) == 0)
    def _():
        y_ref[0:SUBLANES, :] = jnp.zeros((SUBLANES, D_MODEL), F32)

    x = h_ref[...]
    xs, scale = _rms_split(x, g_ref[...])
    c_gate = _dot(xs, w_in_ref[:, D_MODEL:2 * D_MODEL]) * scale
    hid = _dot(xs, w_in_ref[:, 2 * D_MODEL:]) * scale
    y_ref[SUBLANES:, :] = c_gate * hid
    z = k_ref[CONV_WIDTH - 1:CONV_WIDTH, :] * y_ref[SUBLANES:, :]
    for lag in range(1, CONV_WIDTH):
        tap = k_ref[CONV_WIDTH - 1 - lag:CONV_WIDTH - lag, :]
        z = z + tap * y_ref[pl.ds(SUBLANES - lag, CONV_TILE), :]
    y_ref[0:SUBLANES, :] = y_ref[CONV_TILE:, :]
    b_gate = _dot(xs, w_in_ref[:, :D_MODEL]) * scale
    _convert_slabs(next_src, next_dst)
    o_ref[...] = x + _dot((b_gate * z).astype(BF16), w_out_ref[...])


def _conv_layer(h, layer, j, g_mix, w_in, conv_k, w_out, next_weights):
    assert SEQ // CONV_TILE == MIXER_STEPS_PER_ROW
    next_args, next_in, next_out, next_shapes = _next_weight_jobs(next_weights, layer)
    tile = pl.BlockSpec((None, CONV_TILE, D_MODEL), lambda b, s: (b, s, 0))
    out = pl.pallas_call(
        functools.partial(_conv_kernel, j=j),
        grid=(BATCH, SEQ // CONV_TILE),
        in_specs=[tile,
                  _const_spec((1, D_MODEL)),
                  _HBM,
                  _const_spec((CONV_WIDTH, D_MODEL)),
                  _HBM,
                  *next_in],
        out_specs=[tile] + next_out,
        out_shape=[jax.ShapeDtypeStruct(h.shape, F32)] + next_shapes,
        scratch_shapes=[pltpu.VMEM((SUBLANES + CONV_TILE, D_MODEL), F32)]
        + _weight_scratch((D_MODEL, 3 * D_MODEL), (D_MODEL, D_MODEL)) + _STAGE_SCRATCH,
        compiler_params=_params(*_SEQUENTIAL),
        name="conv_layer",
    )(h, g_mix.reshape(1, D_MODEL), w_in, conv_k, w_out, *next_args)
    return out[0], out[1:]


def _xattn_ffn_kernel(h_ref, gx_ref, w_q_ref, mem_ref, gm_ref, w_kv_ref, w_o_ref, gf_ref, w1_ref,
                      w2_ref, gfin_ref, o_ref, kt_ref, v_ref, *, final_norm):
    @pl.when(pl.program_id(1) == 0)
    def _():
        ms, scale = _rms_split(mem_ref[...], gm_ref[...])
        kt_ref[...] = (_dot(ms, w_kv_ref[:, :D_MODEL]) * scale).T.astype(BF16)
        v_ref[...] = (_dot(ms, w_kv_ref[:, D_MODEL:]) * scale).astype(BF16)

    head_cols = [slice(hd * XA_DH, (hd + 1) * XA_DH) for hd in range(XA_HEADS)]

    def query(rows):
        x = h_ref[rows, :]
        xs, scale = _rms_split(x, gx_ref[...])
        return x, (_dot(xs, w_q_ref[...]) * scale).astype(BF16)

    def probs(q, wide=()):
        wide = list(wide)
        out = []
        for cols in head_cols:
            s = _dot(q[:, cols], kt_ref[cols, :]) * (XA_DH ** -0.5)
            if wide:
                wide.pop(0)()
            e = jnp.exp(s - jnp.max(s, axis=-1, keepdims=True))
            out.append((e / jnp.sum(e, axis=-1, keepdims=True)).astype(BF16))
        for thunk in wide:
            thunk()
        return out

    def attend(x, p):
        heads = [_dot(ph, v_ref[:, cols]).astype(BF16) for ph, cols in zip(p, head_cols)]
        return x + _dot(jnp.concatenate(heads, axis=-1), w_o_ref[...])

    def hidden_pieces(y, acts):
        ys, scale = _rms_split(y, gf_ref[...])

        def piece(cols):
            acts.append(jnp.square(jax.nn.relu(_dot(ys, w1_ref[:, cols]) * scale)).astype(BF16))

        return [functools.partial(piece, slice(c * FFN_COLS, (c + 1) * FFN_COLS))
                for c in range(D_FF // FFN_COLS)]

    def finish(rows, y, acts):
        out = y + _dot(jnp.concatenate(acts, axis=-1), w2_ref[...])
        o_ref[rows, :] = _rms(out, gfin_ref[...]) if final_norm else out

    subs = [slice(i * XA_SUB, (i + 1) * XA_SUB) for i in range(XF_TILE // XA_SUB)]
    x, q = query(subs[0])
    p = probs(q)
    for i, rows in enumerate(subs):
        ahead = query(subs[i + 1]) if i + 1 < len(subs) else None
        y = attend(x, p)
        acts = []
        pieces = hidden_pieces(y, acts)
        if ahead is not None:
            x, p = ahead[0], probs(ahead[1], pieces)
        else:
            for piece in pieces:
                piece()
        finish(rows, y, acts)


def _xattn_ffn_layer(h, mem, layer, norm_xa_g, norm_mem_g, norm_ffn_g, norm_f_g, weights,
                     final_norm):
    w_q, w_kv, w_o, w1, w2 = (w.reshape(-1, w.shape[-1]) for w in weights)
    tile = pl.BlockSpec((None, XF_TILE, D_MODEL), lambda b, j: (b, j, 0))
    return pl.pallas_call(
        functools.partial(_xattn_ffn_kernel, final_norm=final_norm),
        grid=(BATCH, SEQ // XF_TILE),
        in_specs=[tile,
                  _layer_spec((1, D_MODEL), layer),
                  _const_spec(w_q.shape),
                  pl.BlockSpec((None, MEM_LEN, D_MODEL), lambda b, j: (b, 0, 0)),
                  _layer_spec((1, D_MODEL), layer),
                  _const_spec(w_kv.shape),
                  _const_spec(w_o.shape),
                  _layer_spec((1, D_MODEL), layer),
                  _const_spec(w1.shape),
                  _const_spec(w2.shape),
                  _const_spec((1, D_MODEL))],
        out_specs=tile,
        out_shape=jax.ShapeDtypeStruct(h.shape, F32),
        scratch_shapes=_weight_scratch((D_MODEL, MEM_LEN), (MEM_LEN, D_MODEL)),
        compiler_params=_params(*_SEQUENTIAL),
        name="xattn_ffn_layer",
    )(h, norm_xa_g.reshape(DEPTH, 1, D_MODEL), w_q, mem, norm_mem_g.reshape(DEPTH, 1, D_MODEL),
      w_kv, w_o, norm_ffn_g.reshape(DEPTH, 1, D_MODEL), w1, w2, norm_f_g.reshape(1, D_MODEL))


def kernel(x, mem, positions, norm_mix_g, norm_xa_g, norm_mem_g, xa_w_q, xa_w_kv, xa_w_o,
           norm_ffn_g, ffn_w1, ffn_w2, ret_w_in, ret_gn_g, ret_w_out,
           gmlp_w_in, gmlp_ln_g, gmlp_ln_b, gmlp_w_s, gmlp_b_s, gmlp_w_out,
           conv_w_in, conv_k, conv_w_out, norm_f_g):
    h = x
    tables = None
    next_weights = (xa_w_q, xa_w_kv, xa_w_o, ffn_w1, ffn_w2)
    for i in range(DEPTH):
        kind, j = i % N_MIXERS, i // N_MIXERS
        if kind == 0:
            h, tables, converted = _retention_layer(h, tables, positions, i, j, norm_mix_g,
                                                    ret_w_in, ret_gn_g, ret_w_out, next_weights)
        elif kind == 1:
            h, converted = _gmlp_layer(h, i, j, norm_mix_g[i], gmlp_w_in, gmlp_ln_g[j],
                                       gmlp_ln_b[j], gmlp_w_s[j], gmlp_b_s[j], gmlp_w_out,
                                       next_weights)
        else:
            h, converted = _conv_layer(h, i, j, norm_mix_g[i], conv_w_in, conv_k[j], conv_w_out,
                                       next_weights)
        h = _xattn_ffn_layer(h, mem, i, norm_xa_g, norm_mem_g, norm_ffn_g, norm_f_g, converted,
                             final_norm=(i == DEPTH - 1))
    return h
```

```python
import functools

import jax
import jax.numpy as jnp
from jax import lax
from jax.experimental import pallas as pl
from jax.experimental.pallas import tpu as pltpu

D_MODEL = 1024
BATCH = 8
SEQ = 2048
DEPTH = 4
N_MIXERS = 3

RET_HEADS = 4
RET_DK = D_MODEL // RET_HEADS
RET_DV = 2 * D_MODEL // RET_HEADS
RET_IN = 6 * D_MODEL
ROPE_BASE = 10000.0
ROPE_HALF = RET_DK // 2

GMLP_HALF = 3 * D_MODEL
GMLP_GROUPS = 4
GMLP_GROUP_DIM = GMLP_HALF // GMLP_GROUPS
GMLP_CHUNK = 128

CONV_WIDTH = 3

MEM_LEN = 256
XA_HEADS = 4
XA_DH = D_MODEL // XA_HEADS

D_FF = 4 * D_MODEL

NORM_EPS = 1e-6
GN_EPS = 1e-5

RET_CHUNK = 256
RET_TILE = 512
GMLP_TILE = 512
GMLP_SUB = 256
CONV_TILE = 512
XF_TILE = 512
XA_SUB = 256
FFN_COLS = 1024
OUT_COLS = 512
SUBLANES = 8
STAGE_ROWS, STAGE_COLS = 512, 1024

VMEM_LIMIT_BYTES = 56 * 1024 * 1024

BF16 = jnp.bfloat16
F32 = jnp.float32


def _dot(a, b):
    return jnp.dot(a, b, preferred_element_type=F32)


def _dot_nt(a, b):
    return lax.dot_general(a, b, (((1,), (1,)), ((), ())), preferred_element_type=F32)


def _dot_tn(a, b):
    return lax.dot_general(a, b, (((0,), (0,)), ((), ())), preferred_element_type=F32)


def _rms(x, g):
    return x * lax.rsqrt(jnp.mean(x * x, axis=-1, keepdims=True) + NORM_EPS) * g


def _rms_split(x, g):
    operand = (x * g).astype(BF16)
    scale = lax.rsqrt(jnp.mean(x * x, axis=-1, keepdims=True) + NORM_EPS)
    return operand, scale


def _const_spec(shape):
    zeros = (0,) * len(shape)
    return pl.BlockSpec(shape, lambda *_: zeros, pipeline_mode=pl.Buffered(1))


def _layer_spec(shape, layer):
    idx = (layer,) + (0,) * len(shape)
    return pl.BlockSpec((None,) + shape, lambda *_: idx, pipeline_mode=pl.Buffered(1))


def _params(*semantics):
    return pltpu.CompilerParams(dimension_semantics=semantics,
                                vmem_limit_bytes=VMEM_LIMIT_BYTES)


_SEQUENTIAL = ("arbitrary", "arbitrary")
_HBM = pl.BlockSpec(memory_space=pl.ANY)
_STAGE_SCRATCH = [pltpu.VMEM((2, STAGE_ROWS, STAGE_COLS), F32), pltpu.SemaphoreType.DMA((2,))]


def _weight_scratch(*shapes):
    return [pltpu.VMEM(shape, BF16) for shape in shapes]


N_NEXT = 5
MIXER_STEPS_PER_ROW = 4


def _next_weight_jobs(weights, layer):
    steps = BATCH * MIXER_STEPS_PER_ROW
    args, in_specs, out_specs, out_shapes = [], [], [], []
    for w in weights:
        depth, rows, cols = w.shape
        slab = rows // steps
        assert slab * steps == rows and slab % (2 * SUBLANES) == 0, w.shape
        args.append(w.reshape(depth, steps, slab, cols))
        in_specs.append(pl.BlockSpec((None, None, slab, cols),
                                     lambda b, s: (layer, b * MIXER_STEPS_PER_ROW + s, 0, 0)))
        out_specs.append(pl.BlockSpec((None, slab, cols),
                                      lambda b, s: (b * MIXER_STEPS_PER_ROW + s, 0, 0)))
        out_shapes.append(jax.ShapeDtypeStruct((steps, slab, cols), BF16))
    return args, in_specs, out_specs, out_shapes


def _convert_slabs(src_refs, dst_refs):
    for src, dst in zip(src_refs, dst_refs):
        dst[...] = src[...].astype(BF16)


def _first_step():
    return jnp.logical_and(pl.program_id(0) == 0, pl.program_id(1) == 0)


def _stage_weights(jobs, stage_ref, sem_ref):
    blocks = []
    for src, dst in jobs:
        rows, cols = dst.shape
        assert rows % STAGE_ROWS == 0 and cols % STAGE_COLS == 0, dst.shape
        blocks += [(src, dst, r, c) for r in range(0, rows, STAGE_ROWS)
                   for c in range(0, cols, STAGE_COLS)]

    def fetch(n):
        src, _, r, c = blocks[n]
        return pltpu.make_async_copy(src.at[pl.ds(r, STAGE_ROWS), pl.ds(c, STAGE_COLS)],
                                     stage_ref.at[n % 2], sem_ref.at[n % 2])

    fetch(0).start()
    for n, (_, dst, r, c) in enumerate(blocks):
        if n + 1 < len(blocks):
            fetch(n + 1).start()
        fetch(n).wait()
        dst[r:r + STAGE_ROWS, c:c + STAGE_COLS] = stage_ref[n % 2].astype(BF16)


def _retention_kernel(decay_ref, h_ref, trig_a_ref, trig_b_ref, g_ref, w_in_hbm, intra_ref,
                      xi_ref, zeta_ref, gn_ref, w_out_hbm, *rest, j, make_tables):
    next_src, (o_ref, *rest) = rest[:N_NEXT], rest[N_NEXT:]
    if make_tables:
        cos_out_ref, sin_out_ref, *rest = rest
    next_dst, rest = rest[:N_NEXT], rest[N_NEXT:]
    state_ref, gated_ref, w_in_ref, w_out_ref, stage_ref, sem_ref = rest

    @pl.when(_first_step())
    def _():
        _stage_weights([(w_in_hbm.at[j], w_in_ref), (w_out_hbm.at[j], w_out_ref)],
                       stage_ref, sem_ref)

    @pl.when(pl.program_id(1) == 0)
    def _():
        state_ref[...] = jnp.zeros_like(state_ref)

    x = h_ref[...]
    xs, scale = _rms_split(x, g_ref[...])

    k_off, v_off, gate_off = D_MODEL, 2 * D_MODEL, 4 * D_MODEL
    if make_tables:
        ang = trig_a_ref[...].astype(F32) * trig_b_ref[...]
        cos_all, sin_all = jnp.cos(ang), jnp.sin(ang)
        cos_out_ref[...] = cos_all
        sin_out_ref[...] = sin_all
    else:
        cos_all, sin_all = trig_a_ref[...], trig_b_ref[...]

    chunks = [slice(c * RET_CHUNK, (c + 1) * RET_CHUNK) for c in range(RET_TILE // RET_CHUNK)]
    out_cols = [slice(n * OUT_COLS, (n + 1) * OUT_COLS) for n in range(D_MODEL // OUT_COLS)]
    half_v = (gate_off - v_off) // 2
    proj = {}

    def project_part(c, part):
        rows = chunks[c]
        if part == 0:
            proj[c, "q"] = _dot(xs[rows], w_in_ref[:, :k_off]) * scale[rows]
        elif part == 1:
            proj[c, "k"] = _dot(xs[rows], w_in_ref[:, k_off:v_off]) * (scale[rows] * RET_DK ** -0.5)
        else:
            lo = v_off + (part - 2) * half_v
            proj[c, "v", part - 2] = (_dot(xs[rows], w_in_ref[:, lo:lo + half_v])
                                      * scale[rows]).astype(BF16)

    def gate_proj(c, hd):
        lo = gate_off + hd * RET_DV
        return _dot(xs[chunks[c]], w_in_ref[:, lo:lo + RET_DV]) * scale[chunks[c]]

    def recur(c, hd, extras):
        rows = chunks[c]
        cos = cos_all[rows]
        sin = sin_all[rows]

        def rotate(t):
            t1, t2 = t[:, :ROPE_HALF], t[:, ROPE_HALF:]
            return jnp.concatenate([t1 * cos - t2 * sin, t2 * cos + t1 * sin], axis=-1)

        dk = slice(hd * RET_DK, (hd + 1) * RET_DK)
        dv = slice(hd * RET_DV, (hd + 1) * RET_DV)
        heads_per_half = RET_HEADS // 2
        v_cols = slice((hd % heads_per_half) * RET_DV, (hd % heads_per_half + 1) * RET_DV)
        q = rotate(proj[c, "q"][:, dk]).astype(BF16)
        k = rotate(proj[c, "k"][:, dk])
        v = proj[c, "v", hd // heads_per_half][:, v_cols]
        state = state_ref[hd]
        s = _dot_nt(q, k.astype(BF16)) * intra_ref[hd]
        gate = gate_proj(c, hd)
        for extra in extras:
            extra()
        o = _dot(s.astype(BF16), v) + _dot(q, state.astype(BF16)) * xi_ref[hd]
        state_ref[hd] = state * decay_ref[hd] + _dot_tn((k * zeta_ref[hd]).astype(BF16), v)
        mu = jnp.mean(o, axis=-1, keepdims=True)
        var = jnp.mean(jnp.square(o - mu), axis=-1, keepdims=True)
        on = (o - mu) * lax.rsqrt(var + GN_EPS) * gn_ref[:, dv]
        gated_ref[rows, dv] = (jax.nn.silu(gate) * on).astype(BF16)

    piece_stride = RET_HEADS // len(out_cols)
    for part in range(RET_HEADS):
        project_part(0, part)
    for c in range(len(chunks)):
        pieces = []
        for hd in range(RET_HEADS):
            extras = []
            if c + 1 < len(chunks):
                extras.append(functools.partial(project_part, c + 1, hd))
            if c > 0 and hd % piece_stride == 0:
                extras.append(lambda cols=out_cols[hd // piece_stride]: pieces.append(
                    _dot(gated_ref[chunks[c - 1], :], w_out_ref[:, cols])))
            recur(c, hd, extras)
        if c > 0:
            o_ref[chunks[c - 1], :] = x[chunks[c - 1]] + jnp.concatenate(pieces, axis=-1)
    last = chunks[-1]
    _convert_slabs(next_src, next_dst)
    o_ref[last, :] = x[last] + _dot(gated_ref[last, :], w_out_ref[...])


def _retention_layer(h, tables, positions, layer, j, norm_mix_g, w_in, gn_g, w_out, next_weights):
    c, t = RET_CHUNK, RET_TILE
    assert SEQ // t == MIXER_STEPS_PER_ROW
    next_args, next_in, next_out, next_shapes = _next_weight_jobs(next_weights, layer)
    log_g = jnp.log(1.0 - 2.0 ** (-5.0 - jnp.arange(RET_HEADS, dtype=F32)))
    idx = jnp.arange(c, dtype=F32)
    rel = idx[:, None] - idx[None, :]
    intra = jnp.where(rel[None] >= 0, jnp.exp(rel[None] * log_g[:, None, None]), 0.0)
    xi = jnp.exp((idx[None] + 1.0) * log_g[:, None])[..., None]
    zeta = jnp.exp((c - 1.0 - idx[None]) * log_g[:, None])[..., None]
    chunk_decay = jnp.exp(c * log_g)

    tile = pl.BlockSpec((None, t, D_MODEL), lambda b, s: (b, s, 0))
    trig = pl.BlockSpec((None, t, ROPE_HALF), lambda b, s: (b, s, 0))
    make_tables = tables is None
    if make_tables:
        inv_freq = ROPE_BASE ** (-jnp.arange(ROPE_HALF, dtype=F32) / ROPE_HALF)
        trig_args = (positions.reshape(BATCH, SEQ, 1), inv_freq.reshape(1, ROPE_HALF))
        trig_specs = [pl.BlockSpec((None, t, 1), lambda b, s: (b, s, 0)),
                      _const_spec((1, ROPE_HALF))]
        table_shape = jax.ShapeDtypeStruct((BATCH, SEQ, ROPE_HALF), F32)
        out_specs = [tile, trig, trig]
        out_shape = [jax.ShapeDtypeStruct(h.shape, F32), table_shape, table_shape]
    else:
        trig_args, trig_specs = tuple(tables), [trig, trig]
        out_specs, out_shape = [tile], [jax.ShapeDtypeStruct(h.shape, F32)]
    out = pl.pallas_call(
        functools.partial(_retention_kernel, j=j, make_tables=make_tables),
        grid=(BATCH, SEQ // t),
        in_specs=[pl.BlockSpec(memory_space=pltpu.SMEM),
                  tile, *trig_specs,
                  _layer_spec((1, D_MODEL), layer),
                  _HBM,
                  _const_spec((RET_HEADS, c, c)),
                  _const_spec((RET_HEADS, c, 1)),
                  _const_spec((RET_HEADS, c, 1)),
                  _layer_spec((1, 2 * D_MODEL), j),
                  _HBM,
                  *next_in],
        out_specs=out_specs + next_out,
        out_shape=out_shape + next_shapes,
        scratch_shapes=[pltpu.VMEM((RET_HEADS, RET_DK, RET_DV), F32),
                        pltpu.VMEM((t, 2 * D_MODEL), BF16)]
        + _weight_scratch((D_MODEL, RET_IN), (2 * D_MODEL, D_MODEL)) + _STAGE_SCRATCH,
        compiler_params=_params(*_SEQUENTIAL),
        name="retention_layer",
    )(chunk_decay, h, *trig_args, norm_mix_g.reshape(DEPTH, 1, D_MODEL), w_in, intra, xi, zeta,
      gn_g.reshape(-1, 1, 2 * D_MODEL), w_out, *next_args)
    own, converted = out[:-N_NEXT], out[-N_NEXT:]
    return own[0], (tuple(own[1:]) if make_tables else tuple(tables)), converted


def _gelu(x):
    return 0.5 * x * (1.0 + lax.erf(x * (2.0 ** -0.5)))


def _gmlp_kernel(h_ref, g_ref, w_in_hbm, ln_g_ref, ln_b_ref, w_s_ref, b_s_ref, w_out_hbm,
                 *rest, j):
    next_src, (o_ref, *rest) = rest[:N_NEXT], rest[N_NEXT:]
    next_dst, rest = rest[:N_NEXT], rest[N_NEXT:]
    v_ref, prod_ref, w_in_ref, w_out_ref, stage_ref, sem_ref = rest

    @pl.when(_first_step())
    def _():
        _stage_weights([(w_in_hbm.at[j], w_in_ref), (w_out_hbm.at[j], w_out_ref)],
                       stage_ref, sem_ref)

    gd = GMLP_GROUP_DIM
    groups = [slice(g * gd, (g + 1) * gd) for g in range(GMLP_GROUPS)]
    row = lax.broadcasted_iota(jnp.int32, (GMLP_CHUNK, GMLP_CHUNK), 0)
    col = lax.broadcasted_iota(jnp.int32, (GMLP_CHUNK, GMLP_CHUNK), 1)
    w_s = [jnp.where(row >= col, w_s_ref[g], 0.0).astype(BF16) for g in range(GMLP_GROUPS)]

    def gate_values(rows):
        x = h_ref[rows, :]
        xs, scale = _rms_split(x, g_ref[...])
        total = jnp.zeros((GMLP_SUB, 1), F32)
        for cols in groups:
            lo = GMLP_HALF + cols.start
            vg = _gelu(_dot(xs, w_in_ref[:, lo:lo + gd]) * scale)
            v_ref[rows, cols] = vg
            total = total + jnp.sum(vg, axis=-1, keepdims=True)
        mu = total * (1.0 / GMLP_HALF)
        sq = jnp.zeros((GMLP_SUB, 1), F32)
        for cols in groups:
            sq = sq + jnp.sum(jnp.square(v_ref[rows, cols] - mu), axis=-1, keepdims=True)
        return x, xs, scale, mu, lax.rsqrt(sq * (1.0 / GMLP_HALF) + GN_EPS)

    def mix(rows, xs, scale, mu, rstd):
        u = _gelu(_dot(xs, w_in_ref[:, groups[0]]) * scale)
        for g, cols in enumerate(groups):
            vn = ((v_ref[rows, cols] - mu) * rstd * ln_g_ref[:, cols] + ln_b_ref[:, cols]).astype(BF16)
            chunk_rows = [slice(ch * GMLP_CHUNK, (ch + 1) * GMLP_CHUNK)
                          for ch in range(GMLP_SUB // GMLP_CHUNK)]
            svs = [_dot(w_s[g], vn[sub]) + b_s_ref[g] for sub in chunk_rows]
            u_next = (_gelu(_dot(xs, w_in_ref[:, groups[g + 1]]) * scale)
                      if g + 1 < len(groups) else None)
            for sub, sv in zip(chunk_rows, svs):
                dst = slice(rows.start + sub.start, rows.start + sub.stop)
                prod_ref[dst, cols] = (u[sub] * sv).astype(BF16)
            u = u_next

    def project(rows, x):
        o_ref[rows, :] = x + _dot(prod_ref[rows, :], w_out_ref[...])

    subs = [slice(i * GMLP_SUB, (i + 1) * GMLP_SUB) for i in range(GMLP_TILE // GMLP_SUB)]
    stats = [gate_values(rows) for rows in subs]
    for rows, (x, xs, scale, mu, rstd) in zip(subs, stats):
        mix(rows, xs, scale, mu, rstd)
        if rows is subs[-1]:
            _convert_slabs(next_src, next_dst)
        project(rows, x)


def _gmlp_layer(h, layer, j, g_mix, w_in, ln_g, ln_b, w_s, b_s, w_out, next_weights):
    assert SEQ // GMLP_TILE == MIXER_STEPS_PER_ROW
    next_args, next_in, next_out, next_shapes = _next_weight_jobs(next_weights, layer)
    tile = pl.BlockSpec((None, GMLP_TILE, D_MODEL), lambda b, s: (b, s, 0))
    out = pl.pallas_call(
        functools.partial(_gmlp_kernel, j=j),
        grid=(BATCH, SEQ // GMLP_TILE),
        in_specs=[tile,
                  _const_spec((1, D_MODEL)),
                  _HBM,
                  _const_spec((1, GMLP_HALF)),
                  _const_spec((1, GMLP_HALF)),
                  _const_spec(w_s.shape),
                  _const_spec((GMLP_GROUPS, GMLP_CHUNK, 1)),
                  _HBM,
                  *next_in],
        out_specs=[tile] + next_out,
        out_shape=[jax.ShapeDtypeStruct(h.shape, F32)] + next_shapes,
        scratch_shapes=[pltpu.VMEM((GMLP_TILE, GMLP_HALF), F32),
                        pltpu.VMEM((GMLP_TILE, GMLP_HALF), BF16)]
        + _weight_scratch((D_MODEL, 2 * GMLP_HALF), (GMLP_HALF, D_MODEL)) + _STAGE_SCRATCH,
        compiler_params=_params(*_SEQUENTIAL),
        name="gmlp_layer",
    )(h, g_mix.reshape(1, D_MODEL), w_in, ln_g.reshape(1, GMLP_HALF), ln_b.reshape(1, GMLP_HALF),
      w_s, b_s[:, :, None], w_out, *next_args)
    return out[0], out[1:]


def _conv_kernel(h_ref, g_ref, w_in_hbm, k_ref, w_out_hbm, *rest, j):
    next_src, (o_ref, *rest) = rest[:N_NEXT], rest[N_NEXT:]
    next_dst, rest = rest[:N_NEXT], rest[N_NEXT:]
    y_ref, w_in_ref, w_out_ref, stage_ref, sem_ref = rest

    @pl.when(_first_step())
    def _():
        _stage_weights([(w_in_hbm.at[j], w_in_ref), (w_out_hbm.at[j], w_out_ref)],
                       stage_ref, sem_ref)

    @pl.when(pl.program_id(1) == 0)
    def _():
        y_ref[0:SUBLANES, :] = jnp.zeros((SUBLANES, D_MODEL), F32)

    x = h_ref[...]
    xs, scale = _rms_split(x, g_ref[...])
    c_gate = _dot(xs, w_in_ref[:, D_MODEL:2 * D_MODEL]) * scale
    hid = _dot(xs, w_in_ref[:, 2 * D_MODEL:]) * scale
    y_ref[SUBLANES:, :] = c_gate * hid
    z = k_ref[CONV_WIDTH - 1:CONV_WIDTH, :] * y_ref[SUBLANES:, :]
    for lag in range(1, CONV_WIDTH):
        tap = k_ref[CONV_WIDTH - 1 - lag:CONV_WIDTH - lag, :]
        z = z + tap * y_ref[pl.ds(SUBLANES - lag, CONV_TILE), :]
    y_ref[0:SUBLANES, :] = y_ref[CONV_TILE:, :]
    b_gate = _dot(xs, w_in_ref[:, :D_MODEL]) * scale
    _convert_slabs(next_src, next_dst)
    o_ref[...] = x + _dot((b_gate * z).astype(BF16), w_out_ref[...])


def _conv_layer(h, layer, j, g_mix, w_in, conv_k, w_out, next_weights):
    assert SEQ // CONV_TILE == MIXER_STEPS_PER_ROW
    next_args, next_in, next_out, next_shapes = _next_weight_jobs(next_weights, layer)
    tile = pl.BlockSpec((None, CONV_TILE, D_MODEL), lambda b, s: (b, s, 0))
    out = pl.pallas_call(
        functools.partial(_conv_kernel, j=j),
        grid=(BATCH, SEQ // CONV_TILE),
        in_specs=[tile,
                  _const_spec((1, D_MODEL)),
                  _HBM,
                  _const_spec((CONV_WIDTH, D_MODEL)),
                  _HBM,
                  *next_in],
        out_specs=[tile] + next_out,
        out_shape=[jax.ShapeDtypeStruct(h.shape, F32)] + next_shapes,
        scratch_shapes=[pltpu.VMEM((SUBLANES + CONV_TILE, D_MODEL), F32)]
        + _weight_scratch((D_MODEL, 3 * D_MODEL), (D_MODEL, D_MODEL)) + _STAGE_SCRATCH,
        compiler_params=_params(*_SEQUENTIAL),
        name="conv_layer",
    )(h, g_mix.reshape(1, D_MODEL), w_in, conv_k, w_out, *next_args)
    return out[0], out[1:]


def _xattn_ffn_kernel(h_ref, gx_ref, w_q_ref, mem_ref, gm_ref, w_kv_ref, w_o_ref, gf_ref, w1_ref,
                      w2_ref, gfin_ref, o_ref, kt_ref, v_ref, *, final_norm):
    @pl.when(pl.program_id(1) == 0)
    def _():
        ms, scale = _rms_split(mem_ref[...], gm_ref[...])
        kt_ref[...] = (_dot(ms, w_kv_ref[:, :D_MODEL]) * scale).T.astype(BF16)
        v_ref[...] = (_dot(ms, w_kv_ref[:, D_MODEL:]) * scale).astype(BF16)

    head_cols = [slice(hd * XA_DH, (hd + 1) * XA_DH) for hd in range(XA_HEADS)]

    def query(rows):
        x = h_ref[rows, :]
        xs, scale = _rms_split(x, gx_ref[...])
        return x, (_dot(xs, w_q_ref[...]) * scale).astype(BF16)

    def probs(q, wide=()):
        wide = list(wide)
        out = []
        for cols in head_cols:
            s = _dot(q[:, cols], kt_ref[cols, :]) * (XA_DH ** -0.5)
            if wide:
                wide.pop(0)()
            e = jnp.exp(s - jnp.max(s, axis=-1, keepdims=True))
            out.append((e / jnp.sum(e, axis=-1, keepdims=True)).astype(BF16))
        for thunk in wide:
            thunk()
        return out

    def attend(x, p):
        heads = [_dot(ph, v_ref[:, cols]).astype(BF16) for ph, cols in zip(p, head_cols)]
        return x + _dot(jnp.concatenate(heads, axis=-1), w_o_ref[...])

    def hidden_pieces(y, acts):
        ys, scale = _rms_split(y, gf_ref[...])

        def piece(cols):
            acts.append(jnp.square(jax.nn.relu(_dot(ys, w1_ref[:, cols]) * scale)).astype(BF16))

        return [functools.partial(piece, slice(c * FFN_COLS, (c + 1) * FFN_COLS))
                for c in range(D_FF // FFN_COLS)]

    def finish(rows, y, acts):
        out = y + _dot(jnp.concatenate(acts, axis=-1), w2_ref[...])
        o_ref[rows, :] = _rms(out, gfin_ref[...]) if final_norm else out

    subs = [slice(i * XA_SUB, (i + 1) * XA_SUB) for i in range(XF_TILE // XA_SUB)]
    x, q = query(subs[0])
    p = probs(q)
    for i, rows in enumerate(subs):
        ahead = query(subs[i + 1]) if i + 1 < len(subs) else None
        y = attend(x, p)
        acts = []
        pieces = hidden_pieces(y, acts)
        if ahead is not None:
            x, p = ahead[0], probs(ahead[1], pieces)
        else:
            for piece in pieces:
                piece()
        finish(rows, y, acts)


def _xattn_ffn_layer(h, mem, layer, norm_xa_g, norm_mem_g, norm_ffn_g, norm_f_g, weights,
                     final_norm):
    w_q, w_kv, w_o, w1, w2 = (w.reshape(-1, w.shape[-1]) for w in weights)
    tile = pl.BlockSpec((None, XF_TILE, D_MODEL), lambda b, j: (b, j, 0))
    return pl.pallas_call(
        functools.partial(_xattn_ffn_kernel, final_norm=final_norm),
        grid=(BATCH, SEQ // XF_TILE),
        in_specs=[tile,
                  _layer_spec((1, D_MODEL), layer),
                  _const_spec(w_q.shape),
                  pl.BlockSpec((None, MEM_LEN, D_MODEL), lambda b, j: (b, 0, 0)),
                  _layer_spec((1, D_MODEL), layer),
                  _const_spec(w_kv.shape),
                  _const_spec(w_o.shape),
                  _layer_spec((1, D_MODEL), layer),
                  _const_spec(w1.shape),
                  _const_spec(w2.shape),
                  _const_spec((1, D_MODEL))],
        out_specs=tile,
        out_shape=jax.ShapeDtypeStruct(h.shape, F32),
        scratch_shapes=_weight_scratch((D_MODEL, MEM_LEN), (MEM_LEN, D_MODEL)),
        compiler_params=_params(*_SEQUENTIAL),
        name="xattn_ffn_layer",
    )(h, norm_xa_g.reshape(DEPTH, 1, D_MODEL), w_q, mem, norm_mem_g.reshape(DEPTH, 1, D_MODEL),
      w_kv, w_o, norm_ffn_g.reshape(DEPTH, 1, D_MODEL), w1, w2, norm_f_g.reshape(1, D_MODEL))


def kernel(x, mem, positions, norm_mix_g, norm_xa_g, norm_mem_g, xa_w_q, xa_w_kv, xa_w_o,
           norm_ffn_g, ffn_w1, ffn_w2, ret_w_in, ret_gn_g, ret_w_out,
           gmlp_w_in, gmlp_ln_g, gmlp_ln_b, gmlp_w_s, gmlp_b_s, gmlp_w_out,
           conv_w_in, conv_k, conv_w_out, norm_f_g):
    h = x
    tables = None
    next_weights = (xa_w_q, xa_w_kv, xa_w_o, ffn_w1, ffn_w2)
    for i in range(DEPTH):
        kind, j = i % N_MIXERS, i // N_MIXERS
        if kind == 0:
            h, tables, converted = _retention_layer(h, tables, positions, i, j, norm_mix_g,
                                                    ret_w_in, ret_gn_g, ret_w_out, next_weights)
        elif kind == 1:
            h, converted = _gmlp_layer(h, i, j, norm_mix_g[i], gmlp_w_in, gmlp_ln_g[j],
                                       gmlp_ln_b[j], gmlp_w_s[j], gmlp_b_s[j], gmlp_w_out,
                                       next_weights)
        else:
            h, converted = _conv_layer(h, i, j, norm_mix_g[i], conv_w_in, conv_k[j], conv_w_out,
                                       next_weights)
        h = _xattn_ffn_layer(h, mem, i, norm_xa_g, norm_mem_g, norm_ffn_g, norm_f_g, converted,
                             final_norm=(i == DEPTH - 1))
    return h
```

```python
import functools

import jax
import jax.numpy as jnp
from jax import lax
from jax.experimental import pallas as pl
from jax.experimental.pallas import tpu as pltpu

D_MODEL = 1024
BATCH = 8
SEQ = 2048
DEPTH = 4
N_MIXERS = 3

RET_HEADS = 4
RET_DK = D_MODEL // RET_HEADS
RET_DV = 2 * D_MODEL // RET_HEADS
RET_IN = 6 * D_MODEL
ROPE_BASE = 10000.0
ROPE_HALF = RET_DK // 2

GMLP_HALF = 3 * D_MODEL
GMLP_GROUPS = 4
GMLP_GROUP_DIM = GMLP_HALF // GMLP_GROUPS
GMLP_CHUNK = 128

CONV_WIDTH = 3

MEM_LEN = 256
XA_HEADS = 4
XA_DH = D_MODEL // XA_HEADS

D_FF = 4 * D_MODEL

NORM_EPS = 1e-6
GN_EPS = 1e-5

RET_CHUNK = 256
RET_TILE = 512
GMLP_TILE = 512
GMLP_SUB = 256
CONV_TILE = 512
XF_TILE = 512
XA_SUB = 256
FFN_COLS = 1024
OUT_COLS = 512
SUBLANES = 8
STAGE_ROWS, STAGE_COLS = 512, 1024

VMEM_LIMIT_BYTES = 56 * 1024 * 1024

BF16 = jnp.bfloat16
F32 = jnp.float32


def _dot(a, b):
    return jnp.dot(a, b, preferred_element_type=F32)


def _dot_nt(a, b):
    return lax.dot_general(a, b, (((1,), (1,)), ((), ())), preferred_element_type=F32)


def _dot_tn(a, b):
    return lax.dot_general(a, b, (((0,), (0,)), ((), ())), preferred_element_type=F32)


def _rms(x, g):
    return x * lax.rsqrt(jnp.mean(x * x, axis=-1, keepdims=True) + NORM_EPS) * g


def _rms_split(x, g):
    operand = (x * g).astype(BF16)
    scale = lax.rsqrt(jnp.mean(x * x, axis=-1, keepdims=True) + NORM_EPS)
    return operand, scale


def _const_spec(shape):
    zeros = (0,) * len(shape)
    return pl.BlockSpec(shape, lambda *_: zeros, pipeline_mode=pl.Buffered(1))


def _layer_spec(shape, layer):
    idx = (layer,) + (0,) * len(shape)
    return pl.BlockSpec((None,) + shape, lambda *_: idx, pipeline_mode=pl.Buffered(1))


def _params(*semantics):
    return pltpu.CompilerParams(dimension_semantics=semantics,
                                vmem_limit_bytes=VMEM_LIMIT_BYTES)


_SEQUENTIAL = ("arbitrary", "arbitrary")
_HBM = pl.BlockSpec(memory_space=pl.ANY)
_STAGE_SCRATCH = [pltpu.VMEM((2, STAGE_ROWS, STAGE_COLS), F32), pltpu.SemaphoreType.DMA((2,))]


def _weight_scratch(*shapes):
    return [pltpu.VMEM(shape, BF16) for shape in shapes]


N_NEXT = 5
MIXER_STEPS_PER_ROW = 4


def _next_weight_jobs(weights, layer):
    steps = BATCH * MIXER_STEPS_PER_ROW
    args, in_specs, out_specs, out_shapes = [], [], [], []
    for w in weights:
        depth, rows, cols = w.shape
        slab = rows // steps
        assert slab * steps == rows and slab % (2 * SUBLANES) == 0, w.shape
        args.append(w.reshape(depth, steps, slab, cols))
        in_specs.append(pl.BlockSpec((None, None, slab, cols),
                                     lambda b, s: (layer, b * MIXER_STEPS_PER_ROW + s, 0, 0)))
        out_specs.append(pl.BlockSpec((None, slab, cols),
                                      lambda b, s: (b * MIXER_STEPS_PER_ROW + s, 0, 0)))
        out_shapes.append(jax.ShapeDtypeStruct((steps, slab, cols), BF16))
    return args, in_specs, out_specs, out_shapes


def _convert_slabs(src_refs, dst_refs):
    for src, dst in zip(src_refs, dst_refs):
        dst[...] = src[...].astype(BF16)


def _first_step():
    return jnp.logical_and(pl.program_id(0) == 0, pl.program_id(1) == 0)


def _stage_weights(jobs, stage_ref, sem_ref):
    blocks = []
    for src, dst in jobs:
        rows, cols = dst.shape
        assert rows % STAGE_ROWS == 0 and cols % STAGE_COLS == 0, dst.shape
        blocks += [(src, dst, r, c) for r in range(0, rows, STAGE_ROWS)
                   for c in range(0, cols, STAGE_COLS)]

    def fetch(n):
        src, _, r, c = blocks[n]
        return pltpu.make_async_copy(src.at[pl.ds(r, STAGE_ROWS), pl.ds(c, STAGE_COLS)],
                                     stage_ref.at[n % 2], sem_ref.at[n % 2])

    fetch(0).start()
    for n, (_, dst, r, c) in enumerate(blocks):
        if n + 1 < len(blocks):
            fetch(n + 1).start()
        fetch(n).wait()
        dst[r:r + STAGE_ROWS, c:c + STAGE_COLS] = stage_ref[n % 2].astype(BF16)


def _retention_kernel(decay_ref, h_ref, trig_a_ref, trig_b_ref, g_ref, w_in_hbm, intra_ref,
                      xi_ref, zeta_ref, gn_ref, w_out_hbm, *rest, j, make_tables):
    next_src, (o_ref, *rest) = rest[:N_NEXT], rest[N_NEXT:]
    if make_tables:
        cos_out_ref, sin_out_ref, *rest = rest
    next_dst, rest = rest[:N_NEXT], rest[N_NEXT:]
    state_ref, gated_ref, w_in_ref, w_out_ref, stage_ref, sem_ref = rest

    @pl.when(_first_step())
    def _():
        _stage_weights([(w_in_hbm.at[j], w_in_ref), (w_out_hbm.at[j], w_out_ref)],
                       stage_ref, sem_ref)

    @pl.when(pl.program_id(1) == 0)
    def _():
        state_ref[...] = jnp.zeros_like(state_ref)

    x = h_ref[...]
    xs, scale = _rms_split(x, g_ref[...])

    k_off, v_off, gate_off = D_MODEL, 2 * D_MODEL, 4 * D_MODEL
    if make_tables:
        ang = trig_a_ref[...].astype(F32) * trig_b_ref[...]
        cos_all, sin_all = jnp.cos(ang), jnp.sin(ang)
        cos_out_ref[...] = cos_all
        sin_out_ref[...] = sin_all
    else:
        cos_all, sin_all = trig_a_ref[...], trig_b_ref[...]

    chunks = [slice(c * RET_CHUNK, (c + 1) * RET_CHUNK) for c in range(RET_TILE // RET_CHUNK)]
    out_cols = [slice(n * OUT_COLS, (n + 1) * OUT_COLS) for n in range(D_MODEL // OUT_COLS)]
    half_v = (gate_off - v_off) // 2
    proj = {}

    def project_part(c, part):
        rows = chunks[c]
        if part == 0:
            proj[c, "q"] = _dot(xs[rows], w_in_ref[:, :k_off]) * scale[rows]
        elif part == 1:
            proj[c, "k"] = _dot(xs[rows], w_in_ref[:, k_off:v_off]) * (scale[rows] * RET_DK ** -0.5)
        else:
            lo = v_off + (part - 2) * half_v
            proj[c, "v", part - 2] = (_dot(xs[rows], w_in_ref[:, lo:lo + half_v])
                                      * scale[rows]).astype(BF16)

    def gate_proj(c, hd):
        lo = gate_off + hd * RET_DV
        return _dot(xs[chunks[c]], w_in_ref[:, lo:lo + RET_DV]) * scale[chunks[c]]

    def recur(c, hd, extras):
        rows = chunks[c]
        cos = cos_all[rows]
        sin = sin_all[rows]

        def rotate(t):
            t1, t2 = t[:, :ROPE_HALF], t[:, ROPE_HALF:]
            return jnp.concatenate([t1 * cos - t2 * sin, t2 * cos + t1 * sin], axis=-1)

        dk = slice(hd * RET_DK, (hd + 1) * RET_DK)
        dv = slice(hd * RET_DV, (hd + 1) * RET_DV)
        heads_per_half = RET_HEADS // 2
        v_cols = slice((hd % heads_per_half) * RET_DV, (hd % heads_per_half + 1) * RET_DV)
        q = rotate(proj[c, "q"][:, dk]).astype(BF16)
        k = rotate(proj[c, "k"][:, dk])
        v = proj[c, "v", hd // heads_per_half][:, v_cols]
        state = state_ref[hd]
        s = _dot_nt(q, k.astype(BF16)) * intra_ref[hd]
        gate = gate_proj(c, hd)
        for extra in extras:
            extra()
        o = _dot(s.astype(BF16), v) + _dot(q, state.astype(BF16)) * xi_ref[hd]
        state_ref[hd] = state * decay_ref[hd] + _dot_tn((k * zeta_ref[hd]).astype(BF16), v)
        mu = jnp.mean(o, axis=-1, keepdims=True)
        var = jnp.mean(jnp.square(o - mu), axis=-1, keepdims=True)
        on = (o - mu) * lax.rsqrt(var + GN_EPS) * gn_ref[:, dv]
        gated_ref[rows, dv] = (jax.nn.silu(gate) * on).astype(BF16)

    piece_stride = RET_HEADS // len(out_cols)
    for part in range(RET_HEADS):
        project_part(0, part)
    for c in range(len(chunks)):
        pieces = []
        for hd in range(RET_HEADS):
            extras = []
            if c + 1 < len(chunks):
                extras.append(functools.partial(project_part, c + 1, hd))
            if c > 0 and hd % piece_stride == 0:
                extras.append(lambda cols=out_cols[hd // piece_stride]: pieces.append(
                    _dot(gated_ref[chunks[c - 1], :], w_out_ref[:, cols])))
            recur(c, hd, extras)
        if c > 0:
            o_ref[chunks[c - 1], :] = x[chunks[c - 1]] + jnp.concatenate(pieces, axis=-1)
    last = chunks[-1]
    _convert_slabs(next_src, next_dst)
    o_ref[last, :] = x[last] + _dot(gated_ref[last, :], w_out_ref[...])


def _retention_layer(h, tables, positions, layer, j, norm_mix_g, w_in, gn_g, w_out, next_weights):
    c, t = RET_CHUNK, RET_TILE
    assert SEQ // t == MIXER_STEPS_PER_ROW
    next_args, next_in, next_out, next_shapes = _next_weight_jobs(next_weights, layer)
    log_g = jnp.log(1.0 - 2.0 ** (-5.0 - jnp.arange(RET_HEADS, dtype=F32)))
    idx = jnp.arange(c, dtype=F32)
    rel = idx[:, None] - idx[None, :]
    intra = jnp.where(rel[None] >= 0, jnp.exp(rel[None] * log_g[:, None, None]), 0.0)
    xi = jnp.exp((idx[None] + 1.0) * log_g[:, None])[..., None]
    zeta = jnp.exp((c - 1.0 - idx[None]) * log_g[:, None])[..., None]
    chunk_decay = jnp.exp(c * log_g)

    tile = pl.BlockSpec((None, t, D_MODEL), lambda b, s: (b, s, 0))
    trig = pl.BlockSpec((None, t, ROPE_HALF), lambda b, s: (b, s, 0))
    make_tables = tables is None
    if make_tables:
        inv_freq = ROPE_BASE ** (-jnp.arange(ROPE_HALF, dtype=F32) / ROPE_HALF)
        trig_args = (positions.reshape(BATCH, SEQ, 1), inv_freq.reshape(1, ROPE_HALF))
        trig_specs = [pl.BlockSpec((None, t, 1), lambda b, s: (b, s, 0)),
                      _const_spec((1, ROPE_HALF))]
        table_shape = jax.ShapeDtypeStruct((BATCH, SEQ, ROPE_HALF), F32)
        out_specs = [tile, trig, trig]
        out_shape = [jax.ShapeDtypeStruct(h.shape, F32), table_shape, table_shape]
    else:
        trig_args, trig_specs = tuple(tables), [trig, trig]
        out_specs, out_shape = [tile], [jax.ShapeDtypeStruct(h.shape, F32)]
    out = pl.pallas_call(
        functools.partial(_retention_kernel, j=j, make_tables=make_tables),
        grid=(BATCH, SEQ // t),
        in_specs=[pl.BlockSpec(memory_space=pltpu.SMEM),
                  tile, *trig_specs,
                  _layer_spec((1, D_MODEL), layer),
                  _HBM,
                  _const_spec((RET_HEADS, c, c)),
                  _const_spec((RET_HEADS, c, 1)),
                  _const_spec((RET_HEADS, c, 1)),
                  _layer_spec((1, 2 * D_MODEL), j),
                  _HBM,
                  *next_in],
        out_specs=out_specs + next_out,
        out_shape=out_shape + next_shapes,
        scratch_shapes=[pltpu.VMEM((RET_HEADS, RET_DK, RET_DV), F32),
                        pltpu.VMEM((t, 2 * D_MODEL), BF16)]
        + _weight_scratch((D_MODEL, RET_IN), (2 * D_MODEL, D_MODEL)) + _STAGE_SCRATCH,
        compiler_params=_params(*_SEQUENTIAL),
        name="retention_layer",
    )(chunk_decay, h, *trig_args, norm_mix_g.reshape(DEPTH, 1, D_MODEL), w_in, intra, xi, zeta,
      gn_g.reshape(-1, 1, 2 * D_MODEL), w_out, *next_args)
    own, converted = out[:-N_NEXT], out[-N_NEXT:]
    return own[0], (tuple(own[1:]) if make_tables else tuple(tables)), converted


def _gelu(x):
    return 0.5 * x * (1.0 + lax.erf(x * (2.0 ** -0.5)))


def _gmlp_kernel(h_ref, g_ref, w_in_hbm, ln_g_ref, ln_b_ref, w_s_ref, b_s_ref, w_out_hbm,
                 *rest, j):
    next_src, (o_ref, *rest) = rest[:N_NEXT], rest[N_NEXT:]
    next_dst, rest = rest[:N_NEXT], rest[N_NEXT:]
    v_ref, prod_ref, w_in_ref, w_out_ref, stage_ref, sem_ref = rest

    @pl.when(_first_step())
    def _():
        _stage_weights([(w_in_hbm.at[j], w_in_ref), (w_out_hbm.at[j], w_out_ref)],
                       stage_ref, sem_ref)

    gd = GMLP_GROUP_DIM
    groups = [slice(g * gd, (g + 1) * gd) for g in range(GMLP_GROUPS)]
    row = lax.broadcasted_iota(jnp.int32, (GMLP_CHUNK, GMLP_CHUNK), 0)
    col = lax.broadcasted_iota(jnp.int32, (GMLP_CHUNK, GMLP_CHUNK), 1)
    w_s = [jnp.where(row >= col, w_s_ref[g], 0.0).astype(BF16) for g in range(GMLP_GROUPS)]

    def gate_values(rows):
        x = h_ref[rows, :]
        xs, scale = _rms_split(x, g_ref[...])
        means, centred = [], []
        for cols in groups:
            lo = GMLP_HALF + cols.start
            vg = _gelu(_dot(xs, w_in_ref[:, lo:lo + gd]) * scale)
            v_ref[rows, cols] = vg
            means.append(jnp.mean(vg, axis=-1, keepdims=True))
            centred.append(jnp.sum(jnp.square(vg - means[-1]), axis=-1, keepdims=True))
        mu = functools.reduce(jnp.add, means) * (1.0 / GMLP_GROUPS)
        sq = functools.reduce(jnp.add, centred)
        for m in means:
            sq = sq + gd * jnp.square(m - mu)
        return x, xs, scale, mu, lax.rsqrt(sq * (1.0 / GMLP_HALF) + GN_EPS)

    def mix(rows, xs, scale, mu, rstd):
        u = _gelu(_dot(xs, w_in_ref[:, groups[0]]) * scale)
        for g, cols in enumerate(groups):
            vn = ((v_ref[rows, cols] - mu) * rstd * ln_g_ref[:, cols] + ln_b_ref[:, cols]).astype(BF16)
            chunk_rows = [slice(ch * GMLP_CHUNK, (ch + 1) * GMLP_CHUNK)
                          for ch in range(GMLP_SUB // GMLP_CHUNK)]
            svs = [_dot(w_s[g], vn[sub]) + b_s_ref[g] for sub in chunk_rows]
            u_next = (_gelu(_dot(xs, w_in_ref[:, groups[g + 1]]) * scale)
                      if g + 1 < len(groups) else None)
            for sub, sv in zip(chunk_rows, svs):
                dst = slice(rows.start + sub.start, rows.start + sub.stop)
                prod_ref[dst, cols] = (u[sub] * sv).astype(BF16)
            u = u_next

    def project(rows, x):
        o_ref[rows, :] = x + _dot(prod_ref[rows, :], w_out_ref[...])

    subs = [slice(i * GMLP_SUB, (i + 1) * GMLP_SUB) for i in range(GMLP_TILE // GMLP_SUB)]
    stats = [gate_values(rows) for rows in subs]
    for rows, (x, xs, scale, mu, rstd) in zip(subs, stats):
        mix(rows, xs, scale, mu, rstd)
        if rows is subs[-1]:
            _convert_slabs(next_src, next_dst)
        project(rows, x)


def _gmlp_layer(h, layer, j, g_mix, w_in, ln_g, ln_b, w_s, b_s, w_out, next_weights):
    assert SEQ // GMLP_TILE == MIXER_STEPS_PER_ROW
    next_args, next_in, next_out, next_shapes = _next_weight_jobs(next_weights, layer)
    tile = pl.BlockSpec((None, GMLP_TILE, D_MODEL), lambda b, s: (b, s, 0))
    out = pl.pallas_call(
        functools.partial(_gmlp_kernel, j=j),
        grid=(BATCH, SEQ // GMLP_TILE),
        in_specs=[tile,
                  _const_spec((1, D_MODEL)),
                  _HBM,
                  _const_spec((1, GMLP_HALF)),
                  _const_spec((1, GMLP_HALF)),
                  _const_spec(w_s.shape),
                  _const_spec((GMLP_GROUPS, GMLP_CHUNK, 1)),
                  _HBM,
                  *next_in],
        out_specs=[tile] + next_out,
        out_shape=[jax.ShapeDtypeStruct(h.shape, F32)] + next_shapes,
        scratch_shapes=[pltpu.VMEM((GMLP_TILE, GMLP_HALF), F32),
                        pltpu.VMEM((GMLP_TILE, GMLP_HALF), BF16)]
        + _weight_scratch((D_MODEL, 2 * GMLP_HALF), (GMLP_HALF, D_MODEL)) + _STAGE_SCRATCH,
        compiler_params=_params(*_SEQUENTIAL),
        name="gmlp_layer",
    )(h, g_mix.reshape(1, D_MODEL), w_in, ln_g.reshape(1, GMLP_HALF), ln_b.reshape(1, GMLP_HALF),
      w_s, b_s[:, :, None], w_out, *next_args)
    return out[0], out[1:]


def _conv_kernel(h_ref, g_ref, w_in_hbm, k_ref, w_out_hbm, *rest, j):
    next_src, (o_ref, *rest) = rest[:N_NEXT], rest[N_NEXT:]
    next_dst, rest = rest[:N_NEXT], rest[N_NEXT:]
    y_ref, w_in_ref, w_out_ref, stage_ref, sem_ref = rest

    @pl.when(_first_step())
    def _():
        _stage_weights([(w_in_hbm.at[j], w_in_ref), (w_out_hbm.at[j], w_out_ref)],
                       stage_ref, sem_ref)

    @pl.when(pl.program_id(1) == 0)
    def _():
        y_ref[0:SUBLANES, :] = jnp.zeros((SUBLANES, D_MODEL), F32)

    x = h_ref[...]
    xs, scale = _rms_split(x, g_ref[...])
    c_gate = _dot(xs, w_in_ref[:, D_MODEL:2 * D_MODEL]) * scale
    hid = _dot(xs, w_in_ref[:, 2 * D_MODEL:]) * scale
    y_ref[SUBLANES:, :] = c_gate * hid
    z = k_ref[CONV_WIDTH - 1:CONV_WIDTH, :] * y_ref[SUBLANES:, :]
    for lag in range(1, CONV_WIDTH):
        tap = k_ref[CONV_WIDTH - 1 - lag:CONV_WIDTH - lag, :]
        z = z + tap * y_ref[pl.ds(SUBLANES - lag, CONV_TILE), :]
    y_ref[0:SUBLANES, :] = y_ref[CONV_TILE:, :]
    b_gate = _dot(xs, w_in_ref[:, :D_MODEL]) * scale
    _convert_slabs(next_src, next_dst)
    o_ref[...] = x + _dot((b_gate * z).astype(BF16), w_out_ref[...])


def _conv_layer(h, layer, j, g_mix, w_in, conv_k, w_out, next_weights):
    assert SEQ // CONV_TILE == MIXER_STEPS_PER_ROW
    next_args, next_in, next_out, next_shapes = _next_weight_jobs(next_weights, layer)
    tile = pl.BlockSpec((None, CONV_TILE, D_MODEL), lambda b, s: (b, s, 0))
    out = pl.pallas_call(
        functools.partial(_conv_kernel, j=j),
        grid=(BATCH, SEQ // CONV_TILE),
        in_specs=[tile,
                  _const_spec((1, D_MODEL)),
                  _HBM,
                  _const_spec((CONV_WIDTH, D_MODEL)),
                  _HBM,
                  *next_in],
        out_specs=[tile] + next_out,
        out_shape=[jax.ShapeDtypeStruct(h.shape, F32)] + next_shapes,
        scratch_shapes=[pltpu.VMEM((SUBLANES + CONV_TILE, D_MODEL), F32)]
        + _weight_scratch((D_MODEL, 3 * D_MODEL), (D_MODEL, D_MODEL)) + _STAGE_SCRATCH,
        compiler_params=_params(*_SEQUENTIAL),
        name="conv_layer",
    )(h, g_mix.reshape(1, D_MODEL), w_in, conv_k, w_out, *next_args)
    return out[0], out[1:]


def _xattn_ffn_kernel(h_ref, gx_ref, w_q_ref, mem_ref, gm_ref, w_kv_ref, w_o_ref, gf_ref, w1_ref,
                      w2_ref, gfin_ref, o_ref, kt_ref, v_ref, *, final_norm):
    @pl.when(pl.program_id(1) == 0)
    def _():
        ms, scale = _rms_split(mem_ref[...], gm_ref[...])
        kt_ref[...] = (_dot(ms, w_kv_ref[:, :D_MODEL]) * scale).T.astype(BF16)
        v_ref[...] = (_dot(ms, w_kv_ref[:, D_MODEL:]) * scale).astype(BF16)

    head_cols = [slice(hd * XA_DH, (hd + 1) * XA_DH) for hd in range(XA_HEADS)]

    def query(rows):
        x = h_ref[rows, :]
        xs, scale = _rms_split(x, gx_ref[...])
        return x, (_dot(xs, w_q_ref[...]) * scale).astype(BF16)

    def probs(q, wide=()):
        wide = list(wide)
        out = []
        for cols in head_cols:
            s = _dot(q[:, cols], kt_ref[cols, :]) * (XA_DH ** -0.5)
            if wide:
                wide.pop(0)()
            e = jnp.exp(s - jnp.max(s, axis=-1, keepdims=True))
            out.append((e / jnp.sum(e, axis=-1, keepdims=True)).astype(BF16))
        for thunk in wide:
            thunk()
        return out

    def attend(x, p):
        heads = [_dot(ph, v_ref[:, cols]).astype(BF16) for ph, cols in zip(p, head_cols)]
        return x + _dot(jnp.concatenate(heads, axis=-1), w_o_ref[...])

    def hidden_pieces(y, acts):
        ys, scale = _rms_split(y, gf_ref[...])

        def piece(cols):
            acts.append(jnp.square(jax.nn.relu(_dot(ys, w1_ref[:, cols]) * scale)).astype(BF16))

        return [functools.partial(piece, slice(c * FFN_COLS, (c + 1) * FFN_COLS))
                for c in range(D_FF // FFN_COLS)]

    def finish(rows, y, acts):
        out = y + _dot(jnp.concatenate(acts, axis=-1), w2_ref[...])
        o_ref[rows, :] = _rms(out, gfin_ref[...]) if final_norm else out

    subs = [slice(i * XA_SUB, (i + 1) * XA_SUB) for i in range(XF_TILE // XA_SUB)]
    x, q = query(subs[0])
    p = probs(q)
    for i, rows in enumerate(subs):
        ahead = query(subs[i + 1]) if i + 1 < len(subs) else None
        y = attend(x, p)
        acts = []
        pieces = hidden_pieces(y, acts)
        if ahead is not None:
            x, p = ahead[0], probs(ahead[1], pieces)
        else:
            for piece in pieces:
                piece()
        finish(rows, y, acts)


def _xattn_ffn_layer(h, mem, layer, norm_xa_g, norm_mem_g, norm_ffn_g, norm_f_g, weights,
                     final_norm):
    w_q, w_kv, w_o, w1, w2 = (w.reshape(-1, w.shape[-1]) for w in weights)
    tile = pl.BlockSpec((None, XF_TILE, D_MODEL), lambda b, j: (b, j, 0))
    return pl.pallas_call(
        functools.partial(_xattn_ffn_kernel, final_norm=final_norm),
        grid=(BATCH, SEQ // XF_TILE),
        in_specs=[tile,
                  _layer_spec((1, D_MODEL), layer),
                  _const_spec(w_q.shape),
                  pl.BlockSpec((None, MEM_LEN, D_MODEL), lambda b, j: (b, 0, 0)),
                  _layer_spec((1, D_MODEL), layer),
                  _const_spec(w_kv.shape),
                  _const_spec(w_o.shape),
                  _layer_spec((1, D_MODEL), layer),
                  _const_spec(w1.shape),
                  _const_spec(w2.shape),
                  _const_spec((1, D_MODEL))],
        out_specs=tile,
        out_shape=jax.ShapeDtypeStruct(h.shape, F32),
        scratch_shapes=_weight_scratch((D_MODEL, MEM_LEN), (MEM_LEN, D_MODEL)),
        compiler_params=_params(*_SEQUENTIAL),
        name="xattn_ffn_layer",
    )(h, norm_xa_g.reshape(DEPTH, 1, D_MODEL), w_q, mem, norm_mem_g.reshape(DEPTH, 1, D_MODEL),
      w_kv, w_o, norm_ffn_g.reshape(DEPTH, 1, D_MODEL), w1, w2, norm_f_g.reshape(1, D_MODEL))


def kernel(x, mem, positions, norm_mix_g, norm_xa_g, norm_mem_g, xa_w_q, xa_w_kv, xa_w_o,
           norm_ffn_g, ffn_w1, ffn_w2, ret_w_in, ret_gn_g, ret_w_out,
           gmlp_w_in, gmlp_ln_g, gmlp_ln_b, gmlp_w_s, gmlp_b_s, gmlp_w_out,
           conv_w_in, conv_k, conv_w_out, norm_f_g):
    h = x
    tables = None
    next_weights = (xa_w_q, xa_w_kv, xa_w_o, ffn_w1, ffn_w2)
    for i in range(DEPTH):
        kind, j = i % N_MIXERS, i // N_MIXERS
        if kind == 0:
            h, tables, converted = _retention_layer(h, tables, positions, i, j, norm_mix_g,
                                                    ret_w_in, ret_gn_g, ret_w_out, next_weights)
        elif kind == 1:
            h, converted = _gmlp_layer(h, i, j, norm_mix_g[i], gmlp_w_in, gmlp_ln_g[j],
                                       gmlp_ln_b[j], gmlp_w_s[j], gmlp_b_s[j], gmlp_w_out,
                                       next_weights)
        else:
            h, converted = _conv_layer(h, i, j, norm_mix_g[i], conv_w_in, conv_k[j], conv_w_out,
                                       next_weights)
        h = _xattn_ffn_layer(h, mem, i, norm_xa_g, norm_mem_g, norm_ffn_g, norm_f_g, converted,
                             final_norm=(i == DEPTH - 1))
    return h
```

```python
import functools

import jax
import jax.numpy as jnp
from jax import lax
from jax.experimental import pallas as pl
from jax.experimental.pallas import tpu as pltpu

D_MODEL = 1024
BATCH = 8
SEQ = 2048
DEPTH = 4
N_MIXERS = 3

RET_HEADS = 4
RET_DK = D_MODEL // RET_HEADS
RET_DV = 2 * D_MODEL // RET_HEADS
RET_IN = 6 * D_MODEL
ROPE_BASE = 10000.0
ROPE_HALF = RET_DK // 2

GMLP_HALF = 3 * D_MODEL
GMLP_GROUPS = 4
GMLP_GROUP_DIM = GMLP_HALF // GMLP_GROUPS
GMLP_CHUNK = 128

CONV_WIDTH = 3

MEM_LEN = 256
XA_HEADS = 4
XA_DH = D_MODEL // XA_HEADS

D_FF = 4 * D_MODEL

NORM_EPS = 1e-6
GN_EPS = 1e-5

RET_CHUNK = 256
RET_TILE = 512
GMLP_TILE = 512
GMLP_SUB = 256
CONV_TILE = 512
XF_TILE = 512
XA_SUB = 256
FFN_COLS = 1024
OUT_COLS = 512
SUBLANES = 8
STAGE_ROWS, STAGE_COLS = 512, 1024

VMEM_LIMIT_BYTES = 56 * 1024 * 1024

BF16 = jnp.bfloat16
F32 = jnp.float32


def _dot(a, b):
    return jnp.dot(a, b, preferred_element_type=F32)


def _dot_nt(a, b):
    return lax.dot_general(a, b, (((1,), (1,)), ((), ())), preferred_element_type=F32)


def _dot_tn(a, b):
    return lax.dot_general(a, b, (((0,), (0,)), ((), ())), preferred_element_type=F32)


def _rms(x, g):
    return x * lax.rsqrt(jnp.mean(x * x, axis=-1, keepdims=True) + NORM_EPS) * g


def _rms_split(x, g):
    operand = (x * g).astype(BF16)
    scale = lax.rsqrt(jnp.mean(x * x, axis=-1, keepdims=True) + NORM_EPS)
    return operand, scale


def _const_spec(shape):
    zeros = (0,) * len(shape)
    return pl.BlockSpec(shape, lambda *_: zeros, pipeline_mode=pl.Buffered(1))


def _layer_spec(shape, layer):
    idx = (layer,) + (0,) * len(shape)
    return pl.BlockSpec((None,) + shape, lambda *_: idx, pipeline_mode=pl.Buffered(1))


def _params(*semantics):
    return pltpu.CompilerParams(dimension_semantics=semantics,
                                vmem_limit_bytes=VMEM_LIMIT_BYTES)


_SEQUENTIAL = ("arbitrary", "arbitrary")
_HBM = pl.BlockSpec(memory_space=pl.ANY)
_STAGE_SCRATCH = [pltpu.VMEM((2, STAGE_ROWS, STAGE_COLS), F32), pltpu.SemaphoreType.DMA((2,))]


def _weight_scratch(*shapes):
    return [pltpu.VMEM(shape, BF16) for shape in shapes]


N_NEXT = 5
MIXER_STEPS_PER_ROW = 4


def _next_weight_jobs(weights, layer):
    steps = BATCH * MIXER_STEPS_PER_ROW
    args, in_specs, out_specs, out_shapes = [], [], [], []
    for w in weights:
        depth, rows, cols = w.shape
        slab = rows // steps
        assert slab * steps == rows and slab % (2 * SUBLANES) == 0, w.shape
        args.append(w.reshape(depth, steps, slab, cols))
        in_specs.append(pl.BlockSpec((None, None, slab, cols),
                                     lambda b, s: (layer, b * MIXER_STEPS_PER_ROW + s, 0, 0)))
        out_specs.append(pl.BlockSpec((None, slab, cols),
                                      lambda b, s: (b * MIXER_STEPS_PER_ROW + s, 0, 0)))
        out_shapes.append(jax.ShapeDtypeStruct((steps, slab, cols), BF16))
    return args, in_specs, out_specs, out_shapes


def _convert_slabs(src_refs, dst_refs):
    for src, dst in zip(src_refs, dst_refs):
        dst[...] = src[...].astype(BF16)


def _first_step():
    return jnp.logical_and(pl.program_id(0) == 0, pl.program_id(1) == 0)


def _stage_weights(jobs, stage_ref, sem_ref):
    blocks = []
    for src, dst in jobs:
        rows, cols = dst.shape
        assert rows % STAGE_ROWS == 0 and cols % STAGE_COLS == 0, dst.shape
        blocks += [(src, dst, r, c) for r in range(0, rows, STAGE_ROWS)
                   for c in range(0, cols, STAGE_COLS)]

    def fetch(n):
        src, _, r, c = blocks[n]
        return pltpu.make_async_copy(src.at[pl.ds(r, STAGE_ROWS), pl.ds(c, STAGE_COLS)],
                                     stage_ref.at[n % 2], sem_ref.at[n % 2])

    fetch(0).start()
    for n, (_, dst, r, c) in enumerate(blocks):
        if n + 1 < len(blocks):
            fetch(n + 1).start()
        fetch(n).wait()
        dst[r:r + STAGE_ROWS, c:c + STAGE_COLS] = stage_ref[n % 2].astype(BF16)


def _retention_kernel(decay_ref, h_ref, trig_a_ref, trig_b_ref, g_ref, w_in_hbm, intra_ref,
                      xi_ref, zeta_ref, gn_ref, w_out_hbm, *rest, j, make_tables):
    next_src, (o_ref, *rest) = rest[:N_NEXT], rest[N_NEXT:]
    if make_tables:
        cos_out_ref, sin_out_ref, *rest = rest
    next_dst, rest = rest[:N_NEXT], rest[N_NEXT:]
    state_ref, gated_ref, w_in_ref, w_out_ref, stage_ref, sem_ref = rest

    @pl.when(_first_step())
    def _():
        _stage_weights([(w_in_hbm.at[j], w_in_ref), (w_out_hbm.at[j], w_out_ref)],
                       stage_ref, sem_ref)

    @pl.when(pl.program_id(1) == 0)
    def _():
        state_ref[...] = jnp.zeros_like(state_ref)

    x = h_ref[...]
    xs, scale = _rms_split(x, g_ref[...])

    k_off, v_off, gate_off = D_MODEL, 2 * D_MODEL, 4 * D_MODEL
    if make_tables:
        ang = trig_a_ref[...].astype(F32) * trig_b_ref[...]
        cos_all, sin_all = jnp.cos(ang), jnp.sin(ang)
        cos_out_ref[...] = cos_all
        sin_out_ref[...] = sin_all
    else:
        cos_all, sin_all = trig_a_ref[...], trig_b_ref[...]

    chunks = [slice(c * RET_CHUNK, (c + 1) * RET_CHUNK) for c in range(RET_TILE // RET_CHUNK)]
    out_cols = [slice(n * OUT_COLS, (n + 1) * OUT_COLS) for n in range(D_MODEL // OUT_COLS)]
    half_v = (gate_off - v_off) // 2
    proj = {}

    def project_part(c, part):
        rows = chunks[c]
        if part == 0:
            proj[c, "q"] = _dot(xs[rows], w_in_ref[:, :k_off]) * scale[rows]
        elif part == 1:
            proj[c, "k"] = _dot(xs[rows], w_in_ref[:, k_off:v_off]) * (scale[rows] * RET_DK ** -0.5)
        else:
            lo = v_off + (part - 2) * half_v
            proj[c, "v", part - 2] = (_dot(xs[rows], w_in_ref[:, lo:lo + half_v])
                                      * scale[rows]).astype(BF16)

    def gate_proj(c, hd):
        lo = gate_off + hd * RET_DV
        return _dot(xs[chunks[c]], w_in_ref[:, lo:lo + RET_DV]) * scale[chunks[c]]

    def recur(c, hd, extras):
        rows = chunks[c]
        cos = cos_all[rows]
        sin = sin_all[rows]

        def rotate(t):
            t1, t2 = t[:, :ROPE_HALF], t[:, ROPE_HALF:]
            return jnp.concatenate([t1 * cos - t2 * sin, t2 * cos + t1 * sin], axis=-1)

        dk = slice(hd * RET_DK, (hd + 1) * RET_DK)
        dv = slice(hd * RET_DV, (hd + 1) * RET_DV)
        heads_per_half = RET_HEADS // 2
        v_cols = slice((hd % heads_per_half) * RET_DV, (hd % heads_per_half + 1) * RET_DV)
        q = rotate(proj[c, "q"][:, dk]).astype(BF16)
        k = rotate(proj[c, "k"][:, dk])
        v = proj[c, "v", hd // heads_per_half][:, v_cols]
        state = state_ref[hd]
        s = _dot_nt(q, k.astype(BF16)) * intra_ref[hd]
        gate = gate_proj(c, hd)
        for extra in extras:
            extra()
        o = _dot(s.astype(BF16), v) + _dot(q, state.astype(BF16)) * xi_ref[hd]
        state_ref[hd] = state * decay_ref[hd] + _dot_tn((k * zeta_ref[hd]).astype(BF16), v)
        mu = jnp.mean(o, axis=-1, keepdims=True)
        var = jnp.mean(jnp.square(o - mu), axis=-1, keepdims=True)
        on = (o - mu) * lax.rsqrt(var + GN_EPS) * gn_ref[:, dv]
        gated_ref[rows, dv] = (jax.nn.silu(gate) * on).astype(BF16)

    piece_stride = RET_HEADS // len(out_cols)
    for part in range(RET_HEADS):
        project_part(0, part)
    for c in range(len(chunks)):
        pieces = []
        for hd in range(RET_HEADS):
            extras = []
            if c + 1 < len(chunks):
                extras.append(functools.partial(project_part, c + 1, hd))
            if c > 0 and hd % piece_stride == 0:
                extras.append(lambda cols=out_cols[hd // piece_stride]: pieces.append(
                    _dot(gated_ref[chunks[c - 1], :], w_out_ref[:, cols])))
            recur(c, hd, extras)
        if c > 0:
            o_ref[chunks[c - 1], :] = x[chunks[c - 1]] + jnp.concatenate(pieces, axis=-1)
    last = chunks[-1]
    _convert_slabs(next_src, next_dst)
    o_ref[last, :] = x[last] + _dot(gated_ref[last, :], w_out_ref[...])


def _retention_layer(h, tables, positions, layer, j, norm_mix_g, w_in, gn_g, w_out, next_weights):
    c, t = RET_CHUNK, RET_TILE
    assert SEQ // t == MIXER_STEPS_PER_ROW
    next_args, next_in, next_out, next_shapes = _next_weight_jobs(next_weights, layer)
    log_g = jnp.log(1.0 - 2.0 ** (-5.0 - jnp.arange(RET_HEADS, dtype=F32)))
    idx = jnp.arange(c, dtype=F32)
    rel = idx[:, None] - idx[None, :]
    intra = jnp.where(rel[None] >= 0, jnp.exp(rel[None] * log_g[:, None, None]), 0.0)
    xi = jnp.exp((idx[None] + 1.0) * log_g[:, None])[..., None]
    zeta = jnp.exp((c - 1.0 - idx[None]) * log_g[:, None])[..., None]
    chunk_decay = jnp.exp(c * log_g)

    tile = pl.BlockSpec((None, t, D_MODEL), lambda b, s: (b, s, 0))
    trig = pl.BlockSpec((None, t, ROPE_HALF), lambda b, s: (b, s, 0))
    make_tables = tables is None
    if make_tables:
        inv_freq = ROPE_BASE ** (-jnp.arange(ROPE_HALF, dtype=F32) / ROPE_HALF)
        trig_args = (positions.reshape(BATCH, SEQ, 1), inv_freq.reshape(1, ROPE_HALF))
        trig_specs = [pl.BlockSpec((None, t, 1), lambda b, s: (b, s, 0)),
                      _const_spec((1, ROPE_HALF))]
        table_shape = jax.ShapeDtypeStruct((BATCH, SEQ, ROPE_HALF), F32)
        out_specs = [tile, trig, trig]
        out_shape = [jax.ShapeDtypeStruct(h.shape, F32), table_shape, table_shape]
    else:
        trig_args, trig_specs = tuple(tables), [trig, trig]
        out_specs, out_shape = [tile], [jax.ShapeDtypeStruct(h.shape, F32)]
    out = pl.pallas_call(
        functools.partial(_retention_kernel, j=j, make_tables=make_tables),
        grid=(BATCH, SEQ // t),
        in_specs=[pl.BlockSpec(memory_space=pltpu.SMEM),
                  tile, *trig_specs,
                  _layer_spec((1, D_MODEL), layer),
                  _HBM,
                  _const_spec((RET_HEADS, c, c)),
                  _const_spec((RET_HEADS, c, 1)),
                  _const_spec((RET_HEADS, c, 1)),
                  _layer_spec((1, 2 * D_MODEL), j),
                  _HBM,
                  *next_in],
        out_specs=out_specs + next_out,
        out_shape=out_shape + next_shapes,
        scratch_shapes=[pltpu.VMEM((RET_HEADS, RET_DK, RET_DV), F32),
                        pltpu.VMEM((t, 2 * D_MODEL), BF16)]
        + _weight_scratch((D_MODEL, RET_IN), (2 * D_MODEL, D_MODEL)) + _STAGE_SCRATCH,
        compiler_params=_params(*_SEQUENTIAL),
        name="retention_layer",
    )(chunk_decay, h, *trig_args, norm_mix_g.reshape(DEPTH, 1, D_MODEL), w_in, intra, xi, zeta,
      gn_g.reshape(-1, 1, 2 * D_MODEL), w_out, *next_args)
    own, converted = out[:-N_NEXT], out[-N_NEXT:]
    return own[0], (tuple(own[1:]) if make_tables else tuple(tables)), converted


def _gelu(x):
    return 0.5 * x * (1.0 + lax.erf(x * (2.0 ** -0.5)))


def _gmlp_kernel(h_ref, g_ref, w_in_hbm, ln_g_ref, ln_b_ref, w_s_ref, b_s_ref, w_out_hbm,
                 *rest, j):
    next_src, (o_ref, *rest) = rest[:N_NEXT], rest[N_NEXT:]
    next_dst, rest = rest[:N_NEXT], rest[N_NEXT:]
    prod_ref, w_in_ref, w_out_ref, stage_ref, sem_ref = rest

    @pl.when(_first_step())
    def _():
        _stage_weights([(w_in_hbm.at[j], w_in_ref), (w_out_hbm.at[j], w_out_ref)],
                       stage_ref, sem_ref)

    gd = GMLP_GROUP_DIM
    groups = [slice(g * gd, (g + 1) * gd) for g in range(GMLP_GROUPS)]
    row = lax.broadcasted_iota(jnp.int32, (GMLP_CHUNK, GMLP_CHUNK), 0)
    col = lax.broadcasted_iota(jnp.int32, (GMLP_CHUNK, GMLP_CHUNK), 1)
    w_s = [jnp.where(row >= col, w_s_ref[g], 0.0).astype(BF16) for g in range(GMLP_GROUPS)]

    def gate_values(rows):
        x = h_ref[rows, :]
        xs, scale = _rms_split(x, g_ref[...])
        values, means, centred = [], [], []
        for cols in groups:
            lo = GMLP_HALF + cols.start
            vg = _gelu(_dot(xs, w_in_ref[:, lo:lo + gd]) * scale)
            values.append(vg)
            means.append(jnp.mean(vg, axis=-1, keepdims=True))
            centred.append(jnp.sum(jnp.square(vg - means[-1]), axis=-1, keepdims=True))
        mu = functools.reduce(jnp.add, means) * (1.0 / GMLP_GROUPS)
        sq = functools.reduce(jnp.add, centred)
        for m in means:
            sq = sq + gd * jnp.square(m - mu)
        return x, xs, scale, values, mu, lax.rsqrt(sq * (1.0 / GMLP_HALF) + GN_EPS)

    def mix(rows, xs, scale, values, mu, rstd):
        u = _gelu(_dot(xs, w_in_ref[:, groups[0]]) * scale)
        for g, cols in enumerate(groups):
            vn = ((values[g] - mu) * rstd * ln_g_ref[:, cols] + ln_b_ref[:, cols]).astype(BF16)
            chunk_rows = [slice(ch * GMLP_CHUNK, (ch + 1) * GMLP_CHUNK)
                          for ch in range(GMLP_SUB // GMLP_CHUNK)]
            svs = [_dot(w_s[g], vn[sub]) + b_s_ref[g] for sub in chunk_rows]
            u_next = (_gelu(_dot(xs, w_in_ref[:, groups[g + 1]]) * scale)
                      if g + 1 < len(groups) else None)
            for sub, sv in zip(chunk_rows, svs):
                dst = slice(rows.start + sub.start, rows.start + sub.stop)
                prod_ref[dst, cols] = (u[sub] * sv).astype(BF16)
            u = u_next

    def project(rows, x):
        o_ref[rows, :] = x + _dot(prod_ref[rows, :], w_out_ref[...])

    subs = [slice(i * GMLP_SUB, (i + 1) * GMLP_SUB) for i in range(GMLP_TILE // GMLP_SUB)]
    stats = [gate_values(rows) for rows in subs]
    for rows, (x, xs, scale, values, mu, rstd) in zip(subs, stats):
        mix(rows, xs, scale, values, mu, rstd)
        if rows is subs[-1]:
            _convert_slabs(next_src, next_dst)
        project(rows, x)


def _gmlp_layer(h, layer, j, g_mix, w_in, ln_g, ln_b, w_s, b_s, w_out, next_weights):
    assert SEQ // GMLP_TILE == MIXER_STEPS_PER_ROW
    next_args, next_in, next_out, next_shapes = _next_weight_jobs(next_weights, layer)
    tile = pl.BlockSpec((None, GMLP_TILE, D_MODEL), lambda b, s: (b, s, 0))
    out = pl.pallas_call(
        functools.partial(_gmlp_kernel, j=j),
        grid=(BATCH, SEQ // GMLP_TILE),
        in_specs=[tile,
                  _const_spec((1, D_MODEL)),
                  _HBM,
                  _const_spec((1, GMLP_HALF)),
                  _const_spec((1, GMLP_HALF)),
                  _const_spec(w_s.shape),
                  _const_spec((GMLP_GROUPS, GMLP_CHUNK, 1)),
                  _HBM,
                  *next_in],
        out_specs=[tile] + next_out,
        out_shape=[jax.ShapeDtypeStruct(h.shape, F32)] + next_shapes,
        scratch_shapes=[pltpu.VMEM((GMLP_TILE, GMLP_HALF), BF16)]
        + _weight_scratch((D_MODEL, 2 * GMLP_HALF), (GMLP_HALF, D_MODEL)) + _STAGE_SCRATCH,
        compiler_params=_params(*_SEQUENTIAL),
        name="gmlp_layer",
    )(h, g_mix.reshape(1, D_MODEL), w_in, ln_g.reshape(1, GMLP_HALF), ln_b.reshape(1, GMLP_HALF),
      w_s, b_s[:, :, None], w_out, *next_args)
    return out[0], out[1:]


def _conv_kernel(h_ref, g_ref, w_in_hbm, k_ref, w_out_hbm, *rest, j):
    next_src, (o_ref, *rest) = rest[:N_NEXT], rest[N_NEXT:]
    next_dst, rest = rest[:N_NEXT], rest[N_NEXT:]
    y_ref, w_in_ref, w_out_ref, stage_ref, sem_ref = rest

    @pl.when(_first_step())
    def _():
        _stage_weights([(w_in_hbm.at[j], w_in_ref), (w_out_hbm.at[j], w_out_ref)],
                       stage_ref, sem_ref)

    @pl.when(pl.program_id(1) == 0)
    def _():
        y_ref[0:SUBLANES, :] = jnp.zeros((SUBLANES, D_MODEL), F32)

    x = h_ref[...]
    xs, scale = _rms_split(x, g_ref[...])
    c_gate = _dot(xs, w_in_ref[:, D_MODEL:2 * D_MODEL]) * scale
    hid = _dot(xs, w_in_ref[:, 2 * D_MODEL:]) * scale
    y_ref[SUBLANES:, :] = c_gate * hid
    z = k_ref[CONV_WIDTH - 1:CONV_WIDTH, :] * y_ref[SUBLANES:, :]
    for lag in range(1, CONV_WIDTH):
        tap = k_ref[CONV_WIDTH - 1 - lag:CONV_WIDTH - lag, :]
        z = z + tap * y_ref[pl.ds(SUBLANES - lag, CONV_TILE), :]
    y_ref[0:SUBLANES, :] = y_ref[CONV_TILE:, :]
    b_gate = _dot(xs, w_in_ref[:, :D_MODEL]) * scale
    _convert_slabs(next_src, next_dst)
    o_ref[...] = x + _dot((b_gate * z).astype(BF16), w_out_ref[...])


def _conv_layer(h, layer, j, g_mix, w_in, conv_k, w_out, next_weights):
    assert SEQ // CONV_TILE == MIXER_STEPS_PER_ROW
    next_args, next_in, next_out, next_shapes = _next_weight_jobs(next_weights, layer)
    tile = pl.BlockSpec((None, CONV_TILE, D_MODEL), lambda b, s: (b, s, 0))
    out = pl.pallas_call(
        functools.partial(_conv_kernel, j=j),
        grid=(BATCH, SEQ // CONV_TILE),
        in_specs=[tile,
                  _const_spec((1, D_MODEL)),
                  _HBM,
                  _const_spec((CONV_WIDTH, D_MODEL)),
                  _HBM,
                  *next_in],
        out_specs=[tile] + next_out,
        out_shape=[jax.ShapeDtypeStruct(h.shape, F32)] + next_shapes,
        scratch_shapes=[pltpu.VMEM((SUBLANES + CONV_TILE, D_MODEL), F32)]
        + _weight_scratch((D_MODEL, 3 * D_MODEL), (D_MODEL, D_MODEL)) + _STAGE_SCRATCH,
        compiler_params=_params(*_SEQUENTIAL),
        name="conv_layer",
    )(h, g_mix.reshape(1, D_MODEL), w_in, conv_k, w_out, *next_args)
    return out[0], out[1:]


def _xattn_ffn_kernel(h_ref, gx_ref, w_q_ref, mem_ref, gm_ref, w_kv_ref, w_o_ref, gf_ref, w1_ref,
                      w2_ref, gfin_ref, o_ref, kt_ref, v_ref, *, final_norm):
    @pl.when(pl.program_id(1) == 0)
    def _():
        ms, scale = _rms_split(mem_ref[...], gm_ref[...])
        kt_ref[...] = (_dot(ms, w_kv_ref[:, :D_MODEL]) * scale).T.astype(BF16)
        v_ref[...] = (_dot(ms, w_kv_ref[:, D_MODEL:]) * scale).astype(BF16)

    head_cols = [slice(hd * XA_DH, (hd + 1) * XA_DH) for hd in range(XA_HEADS)]

    def query(rows):
        x = h_ref[rows, :]
        xs, scale = _rms_split(x, gx_ref[...])
        return x, (_dot(xs, w_q_ref[...]) * scale).astype(BF16)

    def probs(q, wide=()):
        wide = list(wide)
        out = []
        for cols in head_cols:
            s = _dot(q[:, cols], kt_ref[cols, :]) * (XA_DH ** -0.5)
            if wide:
                wide.pop(0)()
            e = jnp.exp(s - jnp.max(s, axis=-1, keepdims=True))
            out.append((e / jnp.sum(e, axis=-1, keepdims=True)).astype(BF16))
        for thunk in wide:
            thunk()
        return out

    def attend(x, p):
        heads = [_dot(ph, v_ref[:, cols]).astype(BF16) for ph, cols in zip(p, head_cols)]
        return x + _dot(jnp.concatenate(heads, axis=-1), w_o_ref[...])

    def hidden_pieces(y, acts):
        ys, scale = _rms_split(y, gf_ref[...])

        def piece(cols):
            acts.append(jnp.square(jax.nn.relu(_dot(ys, w1_ref[:, cols]) * scale)).astype(BF16))

        return [functools.partial(piece, slice(c * FFN_COLS, (c + 1) * FFN_COLS))
                for c in range(D_FF // FFN_COLS)]

    def finish(rows, y, acts):
        out = y + _dot(jnp.concatenate(acts, axis=-1), w2_ref[...])
        o_ref[rows, :] = _rms(out, gfin_ref[...]) if final_norm else out

    subs = [slice(i * XA_SUB, (i + 1) * XA_SUB) for i in range(XF_TILE // XA_SUB)]
    x, q = query(subs[0])
    p = probs(q)
    for i, rows in enumerate(subs):
        ahead = query(subs[i + 1]) if i + 1 < len(subs) else None
        y = attend(x, p)
        acts = []
        pieces = hidden_pieces(y, acts)
        if ahead is not None:
            x, p = ahead[0], probs(ahead[1], pieces)
        else:
            for piece in pieces:
                piece()
        finish(rows, y, acts)


def _xattn_ffn_layer(h, mem, layer, norm_xa_g, norm_mem_g, norm_ffn_g, norm_f_g, weights,
                     final_norm):
    w_q, w_kv, w_o, w1, w2 = (w.reshape(-1, w.shape[-1]) for w in weights)
    tile = pl.BlockSpec((None, XF_TILE, D_MODEL), lambda b, j: (b, j, 0))
    return pl.pallas_call(
        functools.partial(_xattn_ffn_kernel, final_norm=final_norm),
        grid=(BATCH, SEQ // XF_TILE),
        in_specs=[tile,
                  _layer_spec((1, D_MODEL), layer),
                  _const_spec(w_q.shape),
                  pl.BlockSpec((None, MEM_LEN, D_MODEL), lambda b, j: (b, 0, 0)),
                  _layer_spec((1, D_MODEL), layer),
                  _const_spec(w_kv.shape),
                  _const_spec(w_o.shape),
                  _layer_spec((1, D_MODEL), layer),
                  _const_spec(w1.shape),
                  _const_spec(w2.shape),
                  _const_spec((1, D_MODEL))],
        out_specs=tile,
        out_shape=jax.ShapeDtypeStruct(h.shape, F32),
        scratch_shapes=_weight_scratch((D_MODEL, MEM_LEN), (MEM_LEN, D_MODEL)),
        compiler_params=_params(*_SEQUENTIAL),
        name="xattn_ffn_layer",
    )(h, norm_xa_g.reshape(DEPTH, 1, D_MODEL), w_q, mem, norm_mem_g.reshape(DEPTH, 1, D_MODEL),
      w_kv, w_o, norm_ffn_g.reshape(DEPTH, 1, D_MODEL), w1, w2, norm_f_g.reshape(1, D_MODEL))


def kernel(x, mem, positions, norm_mix_g, norm_xa_g, norm_mem_g, xa_w_q, xa_w_kv, xa_w_o,
           norm_ffn_g, ffn_w1, ffn_w2, ret_w_in, ret_gn_g, ret_w_out,
           gmlp_w_in, gmlp_ln_g, gmlp_ln_b, gmlp_w_s, gmlp_b_s, gmlp_w_out,
           conv_w_in, conv_k, conv_w_out, norm_f_g):
    h = x
    tables = None
    next_weights = (xa_w_q, xa_w_kv, xa_w_o, ffn_w1, ffn_w2)
    for i in range(DEPTH):
        kind, j = i % N_MIXERS, i // N_MIXERS
        if kind == 0:
            h, tables, converted = _retention_layer(h, tables, positions, i, j, norm_mix_g,
                                                    ret_w_in, ret_gn_g, ret_w_out, next_weights)
        elif kind == 1:
            h, converted = _gmlp_layer(h, i, j, norm_mix_g[i], gmlp_w_in, gmlp_ln_g[j],
                                       gmlp_ln_b[j], gmlp_w_s[j], gmlp_b_s[j], gmlp_w_out,
                                       next_weights)
        else:
            h, converted = _conv_layer(h, i, j, norm_mix_g[i], conv_w_in, conv_k[j], conv_w_out,
                                       next_weights)
        h = _xattn_ffn_layer(h, mem, i, norm_xa_g, norm_mem_g, norm_ffn_g, norm_f_g, converted,
                             final_norm=(i == DEPTH - 1))
    return h
```
